```python
import math
import jax
import jax.numpy as jnp
from jax import lax
import numpy as np

D_MODEL = 2048
BATCH = 8
SEQ = 4096
DEPTH = 4

GRID_W = 64
CTX_LEN = 256

BRANCH_W = D_MODEL // 2
N_BRANCH = 3
A_HEAD = 64
A_HEADS = BRANCH_W // A_HEAD
A_DECAY_RANK = 64
A_ICLR_RANK = 64
A_GATE_RANK = 128
A_CONV = 3
N_DIR = 2
LNX_EPS = 64e-5
B_WINDOWS = (2, 4, 8, 16)
B_GROUPS = len(B_WINDOWS)
B_GROUP_W = BRANCH_W // B_GROUPS
C_HEAD = 64
C_VHEAD = 2 * C_HEAD
C_HEADS = BRANCH_W // C_VHEAD
Q_BLOCK = 128
ROPE_BASE = 10000.0
D_FF = 256 * math.ceil(8 * D_MODEL / (3 * 256))
N_MOD = 6
SPLIT_SIZES = (3 * BRANCH_W, N_DIR * A_DECAY_RANK, N_DIR * A_ICLR_RANK, A_GATE_RANK, BRANCH_W, 2 * C_HEADS * C_HEAD, 2 * C_HEADS * C_HEAD, C_HEADS * C_VHEAD, N_BRANCH * D_MODEL)
D_IN = sum(SPLIT_SIZES)

kernel_name = 'hybrid_rwkv7_pool_diffattn_dit_block'

F32 = jnp.float32


def _rms(x, g, eps=1e-6):
    xf = x.astype(F32)
    y = xf * lax.rsqrt(jnp.mean(xf * xf, axis=-1, keepdims=True) + eps)
    return (y * g.astype(F32)).astype(x.dtype)


def _split_cols(z):
    idx = [int(i) for i in np.cumsum(SPLIT_SIZES)[:-1]]
    return jnp.split(z, idx, axis=-1)


def _centred_conv(u, w):
    half = A_CONV // 2
    n = u.shape[1]
    up = jnp.pad(u, ((0, 0), (half, half), (0, 0)))
    return sum(up[:, j:j + n] * w[j] for j in range(A_CONV))


def _head_l2norm(u):
    uh = u.astype(F32).reshape(*u.shape[:-1], A_HEADS, A_HEAD)
    uh = uh / jnp.maximum(jnp.linalg.norm(uh, axis=-1, keepdims=True), 1e-12)
    return uh.reshape(u.shape).astype(u.dtype)


def _rwkv_prepare(z_rkv, z_dec, z_iclr, conv_w, w0, w_up, a0, a_up, kk_scale, ka):
    r, k, v = jnp.split(_centred_conv(z_rkv, conv_w), 3, axis=-1)
    kk = _head_l2norm(k * kk_scale)
    zd = z_dec.reshape(*z_dec.shape[:-1], N_DIR, A_DECAY_RANK)
    za = z_iclr.reshape(*z_iclr.shape[:-1], N_DIR, A_ICLR_RANK)
    dirs = []
    for d in range(N_DIR):
        w_raw = (w0[d] + jnp.tanh(zd[..., d, :]) @ w_up[d]).astype(F32)
        decay = jnp.exp(-jnp.exp(-jax.nn.softplus(-w_raw) - 0.5))
        a = jax.nn.sigmoid(a0[d] + za[..., d, :] @ a_up[d])
        key_d = k * (1 + (a - 1) * ka)
        dirs.append((decay, key_d, kk * a))
    return r, v, kk, dirs


def _wkv_scan(r, w, k, v, a, b, state0, reverse):
    bsz, n = r.shape[:2]

    def tm(u):
        return jnp.moveaxis(u.astype(F32).reshape(bsz, n, A_HEADS, A_HEAD), 1, 0)

    def step(s, inp):
        rt, wt, kt, vt, at, bt = inp
        sa = jnp.einsum('bhvk,bhk->bhv', s, at)
        s = s * wt[:, :, None, :] + sa[..., None] * bt[:, :, None, :] + vt[..., None] * kt[:, :, None, :]
        return s, jnp.einsum('bhvk,bhk->bhv', s, rt)

    s_fin, ys = lax.scan(step, state0, (tm(r), tm(w), tm(k), tm(v), tm(a), tm(b)), reverse=reverse)
    return jnp.moveaxis(ys, 0, 1).reshape(bsz, n, BRANCH_W), s_fin


def _rwkv_output(y, r, v, keys, z_gate, r_k, lnx_w, lnx_b, gate_up):
    shp = r.shape

    def heads(u):
        return u.astype(F32).reshape(*u.shape[:-1], A_HEADS, A_HEAD)

    yh = heads(y)
    mu = jnp.mean(yh, axis=-1, keepdims=True)
    var = jnp.mean(jnp.square(yh - mu), axis=-1, keepdims=True)
    yh = (yh - mu) * lax.rsqrt(var + LNX_EPS) * heads(lnx_w) + heads(lnx_b)
    rh, vh, rkh = heads(r), heads(v), heads(r_k)
    bonus = sum(jnp.sum(rh * heads(kd) * rkh, axis=-1, keepdims=True) for kd in keys) * vh
    g = jax.nn.sigmoid(z_gate) @ gate_up
    return ((yh + bonus).reshape(shp) * g.astype(F32)).astype(v.dtype)


def _pool_mixer(u, pool_w, pool_scale):
    bsz, n, _ = u.shape
    ug = u.reshape(bsz, n, B_GROUPS, B_GROUP_W)
    csum = jnp.pad(jnp.cumsum(ug.astype(F32), axis=1), ((0, 0), (1, 0), (0, 0), (0, 0)))
    t = jnp.arange(n)
    outs = []
    for gi, win in enumerate(B_WINDOWS):
        lo = jnp.clip(t - win // 2, 0, n)
        hi = jnp.clip(t + win // 2, 0, n)
        cg = csum[:, :, gi]
        mean = (cg[:, hi] - cg[:, lo]) / (hi - lo).astype(F32)[None, :, None]
        outs.append(mean - ug[:, :, gi].astype(F32))
    pooled = jnp.stack(outs, axis=2).astype(u.dtype)
    mixed = jnp.einsum('btgc,gcd->btgd', pooled, pool_w)
    return mixed.reshape(bsz, n, BRANCH_W) * pool_scale


def _axial_rope_tables(rows):
    n_freq = C_HEAD // 4
    inv = ROPE_BASE ** (-jnp.arange(n_freq, dtype=F32) / n_freq)
    t_row = jnp.repeat(jnp.arange(rows, dtype=F32), GRID_W)
    t_col = jnp.tile(jnp.arange(GRID_W, dtype=F32), rows)
    ang = jnp.stack([t_row[:, None] * inv, t_col[:, None] * inv], axis=1)
    return jnp.cos(ang), jnp.sin(ang)


def _apply_rope(u, cos, sin):
    shp = u.shape
    uf = u.astype(F32).reshape(*shp[:-1], 2, 2, C_HEAD // 4)
    ua, ub = uf[..., 0, :], uf[..., 1, :]
    cs, sn = cos[None, :, None, None], sin[None, :, None, None]
    out = jnp.stack([ua * cs - ub * sn, ua * sn + ub * cs], axis=-2)
    return out.reshape(shp).astype(u.dtype)


def _heads_qk(z):
    return z.reshape(*z.shape[:-1], C_HEADS, 2, C_HEAD)


def _heads_v(z):
    return z.reshape(*z.shape[:-1], C_HEADS, C_VHEAD)


def _diff_lambda(lam_qk, lam_init):
    lq = lam_qk.astype(F32)
    return jnp.exp(jnp.sum(lq[0] * lq[1])) - jnp.exp(jnp.sum(lq[2] * lq[3])) + lam_init


def _diff_attention(q, k, v, lam, subln_g, lam_init):
    bsz, nq = q.shape[:2]
    nblk = nq // Q_BLOCK
    qb = jnp.moveaxis(q.reshape(bsz, nblk, Q_BLOCK, C_HEADS, 2, C_HEAD), 1, 0)
    scale = C_HEAD ** -0.5

    def one(qi):
        s = jnp.einsum('bqhjd,bkhjd->bhjqk', qi, k).astype(F32) * scale
        p = jax.nn.softmax(s, axis=-1)
        amap = p[:, :, 0] - lam * p[:, :, 1]
        return jnp.einsum('bhqk,bkhe->bqhe', amap.astype(v.dtype), v)

    o = jnp.moveaxis(lax.map(one, qb), 0, 1).reshape(bsz, nq, C_HEADS, C_VHEAD)
    o = _rms(o, subln_g, eps=1e-5) * (1 - lam_init)
    return o.reshape(bsz, nq, BRANCH_W)


def _merge(branches, z_mix, w_branch_l, w_out_l):
    gates = jax.nn.sigmoid(z_mix.reshape(*z_mix.shape[:-1], N_BRANCH, D_MODEL))
    acc = sum(gates[..., i, :] * (y @ w_branch_l[i]) for i, y in enumerate(branches))
    return acc @ w_out_l


def _swiglu(h, w_up, w_down):
    g, u = jnp.split(h @ w_up, 2, axis=-1)
    return (jax.nn.silu(g) * u) @ w_down


def setup_inputs(seed: int = 0) -> dict:
    key = jax.random.key(seed)
    ks = jax.random.split(key, 32)
    L = DEPTH

    def nrm(i, shape, scale):
        return jax.random.normal(ks[i], shape, F32) * scale

    def near_one(i, shape):
        return 1.0 + nrm(i, shape, 0.05)

    return {
        'x': nrm(0, (BATCH, SEQ, D_MODEL), 1.0),
        'c': nrm(1, (BATCH, D_MODEL), 1.0),
        'ctx': nrm(2, (BATCH, CTX_LEN, D_MODEL), 1.0),
        'c_ctx': nrm(3, (D_MODEL,), 1.0),
        'w_ada': nrm(4, (L, D_MODEL, N_MOD * D_MODEL), 0.5 * D_MODEL ** -0.5),
        'b_ada': nrm(5, (L, N_MOD * D_MODEL), 0.02),
        'norm_g': near_one(6, (L, 2, D_MODEL)),
        'w_in': nrm(7, (L, D_MODEL, D_IN), D_MODEL ** -0.5),
        'rkv_conv': jnp.array([0.25, 0.5, 0.25], F32)[None, :, None] + nrm(8, (L, A_CONV, 3 * BRANCH_W), 0.1),
        'decay_w0': jax.random.uniform(ks[9], (L, N_DIR, BRANCH_W), F32, -6.0, 1.0),
        'decay_up': nrm(10, (L, N_DIR, A_DECAY_RANK, BRANCH_W), 0.1),
        'iclr_a0': nrm(11, (L, N_DIR, BRANCH_W), 0.5),
        'iclr_up': nrm(12, (L, N_DIR, A_ICLR_RANK, BRANCH_W), 0.1),
        'gate_up': nrm(13, (L, A_GATE_RANK, BRANCH_W), A_GATE_RANK ** -0.5),
        'k_k': 0.85 + nrm(14, (L, BRANCH_W), 0.05),
        'k_a': near_one(15, (L, BRANCH_W)),
        'r_k': nrm(16, (L, BRANCH_W), 0.1),
        'lnx_w': near_one(17, (L, BRANCH_W)),
        'lnx_b': nrm(18, (L, BRANCH_W), 0.02),
        'pool_w': nrm(19, (L, B_GROUPS, B_GROUP_W, B_GROUP_W), B_GROUP_W ** -0.5),
        'pool_scale': near_one(20, (L, BRANCH_W)),
        'lam_qk': nrm(21, (L, 4, C_HEAD), 0.1),
        'subln_g': near_one(22, (L, C_VHEAD)),
        'w_branch': nrm(23, (L, N_BRANCH, BRANCH_W, D_MODEL), BRANCH_W ** -0.5),
        'w_out': nrm(24, (L, D_MODEL, D_MODEL), D_MODEL ** -0.5),
        'w_ffn_in': nrm(25, (L, D_MODEL, 2 * D_FF), D_MODEL ** -0.5),
        'w_ffn_out': nrm(26, (L, D_FF, D_MODEL), D_FF ** -0.5),
        'final_g': near_one(27, (D_MODEL,)),
    }


def reference(x, c, ctx, c_ctx, w_ada, b_ada, norm_g, w_in, rkv_conv, decay_w0, decay_up, iclr_a0, iclr_up, gate_up, k_k, k_a, r_k, lnx_w, lnx_b, pool_w, pool_scale, lam_qk, subln_g, w_branch, w_out, w_ffn_in, w_ffn_out, final_g):
    bsz, n_tok = x.shape[:2]
    rows = n_tok // GRID_W
    cos, sin = _axial_rope_tables(rows)
    silu_c = jax.nn.silu(c)[:, None, :]
    silu_cc = jax.nn.silu(c_ctx)
    xc = ctx
    for l in range(DEPTH):
        ctx_out = l < DEPTH - 1
        lam_init = 0.8 - 0.6 * math.exp(-0.3 * l)
        mod = jnp.split(silu_c @ w_ada[l] + b_ada[l], N_MOD, axis=-1)
        modc = jnp.split(silu_cc @ w_ada[l] + b_ada[l], N_MOD, axis=-1)

        hl = _rms(x, norm_g[l, 0]) * (1 + mod[1]) + mod[0]
        hc = _rms(xc, norm_g[l, 0]) * (1 + modc[1]) + modc[0]
        zl = _split_cols(hl @ w_in[l])
        zc = _split_cols(hc @ w_in[l])

        pa = (rkv_conv[l], decay_w0[l], decay_up[l], iclr_a0[l], iclr_up[l], k_k[l], k_a[l])
        ra_l, va_l, kk_l, dirs_l = _rwkv_prepare(zl[0], zl[1], zl[2], *pa)
        ra_c, va_c, kk_c, dirs_c = _rwkv_prepare(zc[0], zc[1], zc[2], *pa)
        s0 = jnp.zeros((bsz, A_HEADS, A_HEAD, A_HEAD), F32)
        ya_l = 0.0
        ya_c = 0.0
        for d in range(N_DIR):
            rev = d == 1
            dec_c, key_c, b_c = dirs_c[d]
            y_cd, s_ctx = _wkv_scan(ra_c, dec_c, key_c, va_c, -kk_c, b_c, s0, rev)
            dec_l, key_l, b_l = dirs_l[d]
            y_ld, _ = _wkv_scan(ra_l, dec_l, key_l, va_l, -kk_l, b_l, s_ctx, rev)
            ya_l = ya_l + y_ld
            if ctx_out:
                ya_c = ya_c + y_cd
        pout = (r_k[l], lnx_w[l], lnx_b[l], gate_up[l])
        ya_l = _rwkv_output(ya_l, ra_l, va_l, [dd[1] for dd in dirs_l], zl[3], *pout)

        yb_l = _pool_mixer(zl[4], pool_w[l], pool_scale[l])

        lam = _diff_lambda(lam_qk[l], lam_init)
        q_l = _apply_rope(_heads_qk(zl[5]), cos, sin)
        k_l = _apply_rope(_heads_qk(zl[6]), cos, sin)
        k_c = _heads_qk(zc[6])
        v_c = _heads_v(zc[7])
        k_all = jnp.concatenate([k_l, k_c], axis=1)
        v_all = jnp.concatenate([_heads_v(zl[7]), v_c], axis=1)
        yc_l = _diff_attention(q_l, k_all, v_all, lam, subln_g[l], lam_init)

        mix_l = _merge((ya_l, yb_l, yc_l), zl[8], w_branch[l], w_out[l])
        x = x + mod[2] * mix_l
        if ctx_out:
            ya_c = _rwkv_output(ya_c, ra_c, va_c, [dd[1] for dd in dirs_c], zc[3], *pout)
            yb_c = _pool_mixer(zc[4], pool_w[l], pool_scale[l])
            yc_c = _diff_attention(_heads_qk(zc[5]), k_c, v_c, lam, subln_g[l], lam_init)
            xc = xc + modc[2] * _merge((ya_c, yb_c, yc_c), zc[8], w_branch[l], w_out[l])

        h2 = _rms(x, norm_g[l, 1]) * (1 + mod[4]) + mod[3]
        x = x + mod[5] * _swiglu(h2, w_ffn_in[l], w_ffn_out[l])
        if ctx_out:
            h2c = _rms(xc, norm_g[l, 1]) * (1 + modc[4]) + modc[3]
            xc = xc + modc[5] * _swiglu(h2c, w_ffn_in[l], w_ffn_out[l])
    return _rms(x, final_g)
```

```python
import functools
import math

import jax
import jax.numpy as jnp
from jax import lax
from jax.experimental import pallas as pl
from jax.experimental.pallas import tpu as pltpu

F32 = jnp.float32
BF16 = jnp.bfloat16

LANES = 128
SUBLANES = 8
VMEM_LIMIT = 56 * 1024 * 1024

A_HEAD = 64
A_RANK = 64
A_GATE_RANK = 128
LNX_EPS = 64e-5
B_WINDOWS = (2, 4, 8, 16)
C_HEAD = 64
C_VHEAD = 2 * C_HEAD
ROPE_BASE = 10000.0
GRID_W = 64
N_MOD = 6
N_BRANCH = 3
RB = 256
CHUNK = 64
PAIR = 2 * A_HEAD
HALO = SUBLANES


def _cparams(sem):
    return pltpu.CompilerParams(dimension_semantics=sem, vmem_limit_bytes=VMEM_LIMIT)


def _pick(n, cands):
    for c in cands:
        if n % c == 0:
            return c
    raise ValueError(f"no tile for {n} in {cands}")


def _mod_row(blk, nb, n_batch):
    return jnp.where(blk % nb == 0, n_batch, blk // nb)


def _split3(x):
    hi = x.astype(BF16)
    r1 = x - hi.astype(F32)
    mid = r1.astype(BF16)
    lo = (r1 - mid.astype(F32)).astype(BF16)
    return hi, mid, lo


def _dot(a, b):
    return jnp.dot(a, b, preferred_element_type=F32)


def _dot_nt(a, b):
    return lax.dot_general(a, b, (((1,), (1,)), ((), ())), preferred_element_type=F32)


def _dot_tn(a, b):
    return lax.dot_general(a, b, (((0,), (0,)), ((), ())), preferred_element_type=F32)


def _dot_x3(x, e):
    hi, mid, lo = _split3(x)
    return _dot(hi, e) + _dot(mid, e) + _dot(lo, e)


def _dot_e3(e, x):
    hi, mid, lo = _split3(x)
    return _dot(e, hi) + _dot(e, mid) + _dot(e, lo)


def _dot_hl(x, w):
    xh = x.astype(BF16)
    xl = (x - xh.astype(F32)).astype(BF16)
    wh = w.astype(BF16)
    wl = (w - wh.astype(F32)).astype(BF16)
    return _dot(xh, wh) + _dot(xh, wl) + _dot(xl, wh)


def _mod_kernel(c_ref, w_ref, b_ref, o_ref):
    c = c_ref[...]
    s = (c * jax.nn.sigmoid(c)).astype(BF16)
    o_ref[0] = _dot(s, w_ref[0].astype(BF16)) + b_ref[0]


def _mod_tables(c_all, w_ada, b_ada):
    nl, d, n6 = w_ada.shape
    mr = c_all.shape[0]
    tn = _pick(n6, (1024, 512, 256, 128))
    return pl.pallas_call(
        _mod_kernel,
        grid=(nl, n6 // tn),
        in_specs=[
            pl.BlockSpec((mr, d), lambda l, j: (0, 0)),
            pl.BlockSpec((1, d, tn), lambda l, j: (l, 0, j)),
            pl.BlockSpec((1, 1, tn), lambda l, j: (l, 0, j)),
        ],
        out_specs=pl.BlockSpec((1, mr, tn), lambda l, j: (l, 0, j)),
        out_shape=jax.ShapeDtypeStruct((nl, mr, n6), F32),
        compiler_params=_cparams(("parallel", "parallel")),
        name="adaln_table",
    )(c_all, w_ada, b_ada.reshape(nl, 1, n6))


def _norm_kernel(x_ref, g_ref, shift_ref, scale_ref, o_ref, *, nb, n_batch):
    row = _mod_row(pl.program_id(0), nb, n_batch)
    x = x_ref[...]
    y = x * lax.rsqrt(jnp.mean(x * x, axis=-1, keepdims=True) + 1e-6) * g_ref[...]
    y = y * (1.0 + scale_ref[pl.ds(row, 1), :]) + shift_ref[pl.ds(row, 1), :]
    o_ref[...] = y.astype(o_ref.dtype)


def _norm_mod(x, g, mod, shift_idx, scale_idx, nb, n_batch):
    n, d = x.shape
    mr = mod.shape[0]
    return pl.pallas_call(
        functools.partial(_norm_kernel, nb=nb, n_batch=n_batch),
        grid=(n // RB,),
        in_specs=[
            pl.BlockSpec((RB, d), lambda i: (i, 0)),
            pl.BlockSpec((1, d), lambda i: (0, 0)),
            pl.BlockSpec((mr, d), lambda i: (0, shift_idx)),
            pl.BlockSpec((mr, d), lambda i: (0, scale_idx)),
        ],
        out_specs=pl.BlockSpec((RB, d), lambda i: (i, 0)),
        out_shape=jax.ShapeDtypeStruct((n, d), BF16),
        compiler_params=_cparams(("parallel",)),
        name="norm_mod",
    )(x, g.reshape(1, d), mod, mod)


def _final_norm_kernel(x_ref, g_ref, o_ref):
    x = x_ref[...]
    o_ref[...] = x * lax.rsqrt(jnp.mean(x * x, axis=-1, keepdims=True) + 1e-6) * g_ref[...]


def _final_norm(x, g, nb, n_batch):
    n, d = x.shape
    nlb = nb - 1
    return pl.pallas_call(
        _final_norm_kernel,
        grid=(n_batch, nlb),
        in_specs=[
            pl.BlockSpec((RB, d), lambda b, i: (b * nb + i + 1, 0)),
            pl.BlockSpec((1, d), lambda b, i: (0, 0)),
        ],
        out_specs=pl.BlockSpec((RB, d), lambda b, i: (b * nlb + i, 0)),
        out_shape=jax.ShapeDtypeStruct((n_batch * nlb * RB, d), F32),
        compiler_params=_cparams(("parallel", "parallel")),
        name="final_norm",
    )(x, g.reshape(1, d))


def _mm_plain_kernel(a_ref, w_ref, o_ref):
    o_ref[...] = _dot(a_ref[...], w_ref[...]).astype(o_ref.dtype)


def _mm_tiles(n, k, m):
    tm = _pick(n, (1024, 512, 256)) if k <= 2048 else _pick(n, (512, 256))
    tn = _pick(m, (512, 384, 256, 128))
    return tm, tn


def _mm(a, w, out_dtype):
    n, k = a.shape
    m = w.shape[1]
    tm, tn = _mm_tiles(n, k, m)
    return pl.pallas_call(
        _mm_plain_kernel,
        grid=(n // tm, m // tn),
        in_specs=[
            pl.BlockSpec((tm, k), lambda i, j: (i, 0)),
            pl.BlockSpec((k, tn), lambda i, j: (0, j)),
        ],
        out_specs=pl.BlockSpec((tm, tn), lambda i, j: (i, j)),
        out_shape=jax.ShapeDtypeStruct((n, m), out_dtype),
        compiler_params=_cparams(("parallel", "parallel")),
        name="mm_plain",
    )(a, w)


def _mm_rope_kernel(a_ref, w_ref, cos_ref, sin_ref, o_ref, *, scale):
    acc = _dot(a_ref[...], w_ref[...])
    lane = lax.broadcasted_iota(jnp.int32, (1, LANES), 1)
    first_half = (lane % (C_HEAD // 2)) < (C_HEAD // 4)
    cos = cos_ref[...]
    sin = sin_ref[...]
    for s in range(acc.shape[1] // LANES):
        u = acc[:, s * LANES:(s + 1) * LANES]
        partner = jnp.where(first_half,
                            pltpu.roll(u, LANES - C_HEAD // 4, 1),
                            pltpu.roll(u, C_HEAD // 4, 1))
        o_ref[:, s * LANES:(s + 1) * LANES] = ((u * cos + partner * sin) * scale).astype(o_ref.dtype)


def _mm_rope(a, w, cos_t, sin_t, scale, out_dtype):
    n, k = a.shape
    m = w.shape[1]
    tm, tn = _mm_tiles(n, k, m)
    return pl.pallas_call(
        functools.partial(_mm_rope_kernel, scale=scale),
        grid=(n // tm, m // tn),
        in_specs=[
            pl.BlockSpec((tm, k), lambda i, j: (i, 0)),
            pl.BlockSpec((k, tn), lambda i, j: (0, j)),
            pl.BlockSpec((tm, LANES), lambda i, j: (i, 0)),
            pl.BlockSpec((tm, LANES), lambda i, j: (i, 0)),
        ],
        out_specs=pl.BlockSpec((tm, tn), lambda i, j: (i, j)),
        out_shape=jax.ShapeDtypeStruct((n, m), out_dtype),
        compiler_params=_cparams(("parallel", "parallel")),
        name="mm_rope",
    )(a, w, cos_t, sin_t)


def _mm_swiglu_kernel(a_ref, wg_ref, wu_ref, o_ref):
    a = a_ref[...]
    g = _dot(a, wg_ref[...])
    u = _dot(a, wu_ref[...])
    o_ref[...] = (g * jax.nn.sigmoid(g) * u).astype(o_ref.dtype)


def _mm_swiglu(a, w, out_dtype):
    n, k = a.shape
    f = w.shape[1] // 2
    tm, tn = _mm_tiles(n, k, f)
    nj = f // tn
    return pl.pallas_call(
        _mm_swiglu_kernel,
        grid=(n // tm, nj),
        in_specs=[
            pl.BlockSpec((tm, k), lambda i, j: (i, 0)),
            pl.BlockSpec((k, tn), lambda i, j: (0, j)),
            pl.BlockSpec((k, tn), lambda i, j: (0, j + nj)),
        ],
        out_specs=pl.BlockSpec((tm, tn), lambda i, j: (i, j)),
        out_shape=jax.ShapeDtypeStruct((n, f), out_dtype),
        compiler_params=_cparams(("parallel", "parallel")),
        name="mm_swiglu",
    )(a, w, w)


def _mm_res_kernel(a_ref, w_ref, res_ref, gate_ref, o_ref, *, nb, n_batch):
    acc = _dot(a_ref[...], w_ref[...])
    tm = acc.shape[0]
    for s in range(tm // RB):
        row = _mod_row(pl.program_id(0) * (tm // RB) + s, nb, n_batch)
        gate = gate_ref[pl.ds(row, 1), :]
        rows = slice(s * RB, (s + 1) * RB)
        o_ref[rows, :] = res_ref[rows, :] + gate * acc[rows, :]


def _mm_res(a, w, res, mod, gate_idx, nb, n_batch):
    n, k = a.shape
    m = w.shape[1]
    mr = mod.shape[0]
    tm, tn = _mm_tiles(n, k, m)
    goff = gate_idx * (m // tn)
    return pl.pallas_call(
        functools.partial(_mm_res_kernel, nb=nb, n_batch=n_batch),
        grid=(n // tm, m // tn),
        in_specs=[
            pl.BlockSpec((tm, k), lambda i, j: (i, 0)),
            pl.BlockSpec((k, tn), lambda i, j: (0, j)),
            pl.BlockSpec((tm, tn), lambda i, j: (i, j)),
            pl.BlockSpec((mr, tn), lambda i, j: (0, goff + j)),
        ],
        out_specs=pl.BlockSpec((tm, tn), lambda i, j: (i, j)),
        out_shape=jax.ShapeDtypeStruct((n, m), F32),
        compiler_params=_cparams(("parallel", "parallel")),
        name="mm_residual",
    )(a, w, res, mod)


def _merge_kernel(ya_ref, yb_ref, yc_ref, w_ref, za_ref, zb_ref, zc_ref, o_ref):
    acc = jax.nn.sigmoid(za_ref[...]) * _dot(ya_ref[...], w_ref[0])
    acc += jax.nn.sigmoid(zb_ref[...]) * _dot(yb_ref[...], w_ref[1])
    acc += jax.nn.sigmoid(zc_ref[...]) * _dot(yc_ref[...], w_ref[2])
    o_ref[...] = acc.astype(o_ref.dtype)


def _merge(ya, yb, yc, w_branch, z_mix):
    n, kw = ya.shape
    d = w_branch.shape[2]
    tm = _pick(n, (1024, 512, 256))
    tn = _pick(d, (512, 256, 128))
    nj = d // tn
    yspec = pl.BlockSpec((tm, kw), lambda i, j: (i, 0))
    return pl.pallas_call(
        _merge_kernel,
        grid=(n // tm, nj),
        in_specs=[
            yspec, yspec, yspec,
            pl.BlockSpec((N_BRANCH, kw, tn), lambda i, j: (0, 0, j)),
            pl.BlockSpec((tm, tn), lambda i, j: (i, j)),
            pl.BlockSpec((tm, tn), lambda i, j: (i, nj + j)),
            pl.BlockSpec((tm, tn), lambda i, j: (i, 2 * nj + j)),
        ],
        out_specs=pl.BlockSpec((tm, tn), lambda i, j: (i, j)),
        out_shape=jax.ShapeDtypeStruct((n, d), BF16),
        compiler_params=_cparams(("parallel", "parallel")),
        name="merge",
    )(ya, yb, yc, w_branch, z_mix, z_mix, z_mix)


def _prep_kernel(zr_ref, zp_ref, zn_ref, zs_ref, conv_ref, w0_ref, wup_ref, a0_ref, aup_ref,
                 kk_ref, ka_ref, e_ref,
                 r_ref, v_ref, nkk_ref, kd_ref, bd_ref, lw_ref, pad_ref, *, seq, tc):
    rt = zr_ref.shape[0]
    aw = r_ref.shape[1]
    pos = (pl.program_id(0) * rt) % seq
    prev_ok = jnp.logical_and(pos != 0, pos != tc)
    end = pos + rt
    next_ok = jnp.logical_and(end != seq, end != tc)
    pad_ref[0:HALO, :] = jnp.where(prev_ok, zp_ref[...], 0.0)
    pad_ref[HALO:HALO + rt, :] = zr_ref[...]
    pad_ref[HALO + rt:HALO + rt + HALO, :] = jnp.where(next_ok, zn_ref[...], 0.0)
    c = (pad_ref[HALO - 1:HALO - 1 + rt, :] * conv_ref[0:1, :]
         + pad_ref[HALO:HALO + rt, :] * conv_ref[1:2, :]
         + pad_ref[HALO + 1:HALO + 1 + rt, :] * conv_ref[2:3, :])
    r = c[:, 0:aw]
    k = c[:, aw:2 * aw]
    v = c[:, 2 * aw:3 * aw]
    r_ref[...] = r
    v_ref[...] = v
    kx = k * kk_ref[...]
    norm = jnp.sqrt(_dot_x3(kx * kx, e_ref[...]))
    kk = kx / jnp.maximum(norm, 1e-12)
    nkk_ref[...] = -kk
    zs = zs_ref[...]
    ka = ka_ref[...]
    for d in range(2):
        zd = zs[:, d * A_RANK:(d + 1) * A_RANK]
        w_raw = w0_ref[d:d + 1, :] + _dot_hl(jnp.tanh(zd), wup_ref[d])
        lw_ref[d] = -math.exp(-0.5) * jax.nn.sigmoid(w_raw)
        za = zs[:, 2 * A_RANK + d * A_RANK:2 * A_RANK + (d + 1) * A_RANK]
        a = jax.nn.sigmoid(a0_ref[d:d + 1, :] + _dot_hl(za, aup_ref[d]))
        kd_ref[d] = k * (1.0 + (a - 1.0) * ka)
        bd_ref[d] = kk * a


def _rwkv_prepare(z_rkv, z_small, conv_w, w0, w_up, a0, a_up, k_k, k_a, e_head, seq, tc):
    n, w3 = z_rkv.shape
    aw = w3 // 3
    rt = 128
    nh = rt // HALO
    last = n // HALO - 1
    full2 = lambda i: (0, 0)
    full3 = lambda i: (0, 0, 0)
    row = jax.ShapeDtypeStruct((n, aw), F32)
    drow = jax.ShapeDtypeStruct((2, n, aw), F32)
    return pl.pallas_call(
        functools.partial(_prep_kernel, seq=seq, tc=tc),
        grid=(n // rt,),
        in_specs=[
            pl.BlockSpec((rt, w3), lambda i: (i, 0)),
            pl.BlockSpec((HALO, w3), lambda i: (jnp.maximum(i * nh - 1, 0), 0)),
            pl.BlockSpec((HALO, w3), lambda i: (jnp.minimum((i + 1) * nh, last), 0)),
            pl.BlockSpec((rt, z_small.shape[1]), lambda i: (i, 0)),
            pl.BlockSpec(conv_w.shape, full2),
            pl.BlockSpec(w0.shape, full2),
            pl.BlockSpec(w_up.shape, full3),
            pl.BlockSpec(a0.shape, full2),
            pl.BlockSpec(a_up.shape, full3),
            pl.BlockSpec((1, aw), full2),
            pl.BlockSpec((1, aw), full2),
            pl.BlockSpec(e_head.shape, full2),
        ],
        out_specs=[
            pl.BlockSpec((rt, aw), lambda i: (i, 0)),
            pl.BlockSpec((rt, aw), lambda i: (i, 0)),
            pl.BlockSpec((rt, aw), lambda i: (i, 0)),
            pl.BlockSpec((2, rt, aw), lambda i: (0, i, 0)),
            pl.BlockSpec((2, rt, aw), lambda i: (0, i, 0)),
            pl.BlockSpec((2, rt, aw), lambda i: (0, i, 0)),
        ],
        out_shape=[row, row, row, drow, drow, drow],
        scratch_shapes=[pltpu.VMEM((rt + 2 * HALO, w3), F32)],
        compiler_params=_cparams(("parallel",)),
        name="rwkv_prepare",
    )(z_rkv, z_rkv, z_rkv, z_small, conv_w, w0, w_up, a0, a_up,
      k_k.reshape(1, aw), k_a.reshape(1, aw), e_head)


def _wkv_kernel(r_ref, v_ref, a_ref, k_ref, b_ref, lw_ref, y_ref, s_ref):
    d = pl.program_id(1)
    fwd = d == 0
    c = r_ref.shape[0]
    c2 = 2 * c
    n_pair = r_ref.shape[1] // PAIR

    @pl.when(pl.program_id(2) == 0)
    def _():
        s_ref[...] = jnp.zeros_like(s_ref)

    sgn = jnp.where(fwd, 1, -1)
    ti = lax.broadcasted_iota(jnp.int32, (c, c), 0)
    si = lax.broadcasted_iota(jnp.int32, (c, c), 1)
    cum = jnp.where((si - ti) * sgn <= 0, 1.0, 0.0).astype(BF16)
    cs_all = _dot_e3(cum, lw_ref[...])
    row_c = lax.broadcasted_iota(jnp.int32, (c, 1), 0)
    last_row = row_c == jnp.where(fwd, c - 1, 0)

    i2 = lax.broadcasted_iota(jnp.int32, (c2, c2), 0)
    j2 = lax.broadcasted_iota(jnp.int32, (c2, c2), 1)
    order = (j2 % c - i2 % c) * sgn
    strict = order < 0
    incl = order <= 0
    xr = i2 ^ j2
    eye = (i2 == j2).astype(F32)
    lane = lax.broadcasted_iota(jnp.int32, (1, PAIR), 1)
    m0 = (lane < A_HEAD).astype(F32)
    m1 = 1.0 - m0

    def stack(x):
        return jnp.concatenate([x * m0, x * m1], axis=0)

    for p in range(n_pair):
        cols = slice(p * PAIR, (p + 1) * PAIR)
        lw = lw_ref[:, cols]
        cs = cs_all[:, cols]
        cs_end = jnp.sum(jnp.where(last_row, cs, 0.0), axis=0, keepdims=True)
        g_in = jnp.exp(cs)
        g_ex = jnp.exp(cs - lw)
        g_inv = jnp.exp(-cs)
        g_end = jnp.exp(cs_end - cs)
        kd = k_ref[:, cols]
        bd = b_ref[:, cols]
        a_t = stack(a_ref[:, cols] * g_ex)
        r_t = stack(r_ref[:, cols] * g_in)
        b_t = stack(bd * g_inv)
        k_t = stack(kd * g_inv)
        v_s = stack(v_ref[:, cols]).astype(BF16)
        bk_end = jnp.concatenate([stack(bd * g_end), stack(kd * g_end)], axis=0).astype(BF16)
        zl = jnp.concatenate([a_t, r_t], axis=0).astype(BF16)
        zr = jnp.concatenate([b_t, k_t], axis=0).astype(BF16)
        aa = _dot_nt(zl, zr)
        a_ab = jnp.where(strict, aa[0:c2, 0:c2], 0.0)
        a_ak = jnp.where(strict, aa[0:c2, c2:2 * c2], 0.0)
        a_rb = jnp.where(incl, aa[c2:2 * c2, 0:c2], 0.0)
        a_rk = jnp.where(incl, aa[c2:2 * c2, c2:2 * c2], 0.0)

        n1 = jnp.where(xr < 8, a_ab, 0.0).astype(BF16)
        n2 = _dot(n1, n1)
        t = eye + n1.astype(F32)
        t = t + _dot(t.astype(BF16), n2.astype(BF16))
        n4 = _dot(n2.astype(BF16), n2.astype(BF16))
        t = t + _dot(t.astype(BF16), n4.astype(BF16))
        m = 8
        while m < c:
            off = jnp.where(jnp.logical_and(xr >= m, xr < 2 * m), a_ab, 0.0).astype(BF16)
            tb = t.astype(BF16)
            t = t + _dot(tb, _dot(off, tb).astype(BF16))
            m *= 2

        s0 = s_ref[p]
        zs = _dot_nt(zl, s0.astype(BF16))
        av = _dot(jnp.concatenate([a_ak, a_rk], axis=0).astype(BF16), v_s)
        u = _dot(t.astype(BF16), (zs[0:c2] + av[0:c2]).astype(BF16))
        y_s = zs[c2:2 * c2] + av[c2:2 * c2] + _dot(a_rb.astype(BF16), u.astype(BF16))
        y_ref[:, cols] = y_s[0:c] + y_s[c:c2]
        uv = jnp.concatenate([u.astype(BF16), v_s], axis=0)
        s_ref[p] = s0 * jnp.exp(cs_end) + _dot_tn(uv, bk_end)


def _wkv(r, v, nkk, kd, bd, lw, n_batch, seq, tc):
    n, aw = r.shape
    c = CHUNK
    nch = seq // c
    ncc = tc // c

    def blk(b, d, i):
        back = jnp.where(i < ncc, ncc - 1 - i, nch - 1 - (i - ncc))
        return b * nch + jnp.where(d == 0, i, back)

    shared = pl.BlockSpec((c, aw), lambda b, d, i: (blk(b, d, i), 0))
    perdir = pl.BlockSpec((None, c, aw), lambda b, d, i: (d, blk(b, d, i), 0))
    return pl.pallas_call(
        _wkv_kernel,
        grid=(n_batch, 2, nch),
        in_specs=[shared, shared, shared, perdir, perdir, perdir],
        out_specs=perdir,
        out_shape=jax.ShapeDtypeStruct((2, n, aw), F32),
        scratch_shapes=[pltpu.VMEM((aw // PAIR, PAIR, PAIR), F32)],
        compiler_params=_cparams(("parallel", "parallel", "arbitrary")),
        name="wkv_scan",
    )(r, v, nkk, kd, bd, lw)


def _rwkv_out_kernel(y_ref, r_ref, v_ref, kd_ref, zs_ref, rk_ref, lw_ref, lb_ref, gup_ref, e_ref, o_ref):
    e = e_ref[...]
    inv = 1.0 / A_HEAD
    y = y_ref[0] + y_ref[1]
    mu = _dot_x3(y, e) * inv
    yc = y - mu
    var = _dot_x3(yc * yc, e) * inv
    yh = yc * lax.rsqrt(var + LNX_EPS) * lw_ref[...] + lb_ref[...]
    bonus = _dot_x3(r_ref[...] * (kd_ref[0] + kd_ref[1]) * rk_ref[...], e) * v_ref[...]
    zg = zs_ref[...][:, 2 * 2 * A_RANK:2 * 2 * A_RANK + A_GATE_RANK]
    g = _dot_hl(jax.nn.sigmoid(zg), gup_ref[...])
    o_ref[...] = ((yh + bonus) * g).astype(o_ref.dtype)


def _rwkv_output(y, r, v, kd, z_small, r_k, lnx_w, lnx_b, gate_up, e_head):
    n, aw = r.shape
    rt = RB
    rowspec = pl.BlockSpec((rt, aw), lambda i: (i, 0))
    dspec = pl.BlockSpec((2, rt, aw), lambda i: (0, i, 0))
    vec = pl.BlockSpec((1, aw), lambda i: (0, 0))
    return pl.pallas_call(
        _rwkv_out_kernel,
        grid=(n // rt,),
        in_specs=[
            dspec, rowspec, rowspec, dspec,
            pl.BlockSpec((rt, z_small.shape[1]), lambda i: (i, 0)),
            vec, vec, vec,
            pl.BlockSpec(gate_up.shape, lambda i: (0, 0)),
            pl.BlockSpec(e_head.shape, lambda i: (0, 0)),
        ],
        out_specs=rowspec,
        out_shape=jax.ShapeDtypeStruct((n, aw), BF16),
        compiler_params=_cparams(("parallel",)),
        name="rwkv_output",
    )(y, r, v, kd, z_small, r_k.reshape(1, aw), lnx_w.reshape(1, aw), lnx_b.reshape(1, aw),
      gate_up, e_head)


def _pool_kernel(u_ref, w_ref, ps_ref, o_ref, pad_ref, *, tc):
    seq = u_ref.shape[0]
    t_lat = seq - tc
    half_max = max(B_WINDOWS) // 2
    lat0 = half_max + tc + 2 * half_max
    gw = u_ref.shape[1]
    zeros = jnp.zeros((2 * half_max, gw), F32)
    pad_ref[0:half_max, :] = zeros[0:half_max]
    pad_ref[half_max:half_max + tc, :] = u_ref[0:tc, :]
    pad_ref[half_max + tc:lat0, :] = zeros
    pad_ref[lat0:lat0 + t_lat, :] = u_ref[tc:seq, :]
    pad_ref[lat0 + t_lat:lat0 + t_lat + half_max, :] = zeros[0:half_max]

    def segment(start, n, win):
        half = win // 2
        acc = pad_ref[start - half:start - half + n, :]
        for j in range(1 - half, half):
            acc = acc + pad_ref[start + j:start + j + n, :]
        t = lax.broadcasted_iota(jnp.int32, (n, gw), 0)
        cnt = jnp.minimum(t + half, n) - jnp.maximum(t - half, 0)
        return acc / cnt.astype(F32) - pad_ref[start:start + n, :]

    for gi, win in enumerate(B_WINDOWS):
        @pl.when(pl.program_id(1) == gi)
        def _(win=win):
            w = w_ref[0]
            ps = ps_ref[...]
            o_ref[0:tc, :] = (_dot(segment(half_max, tc, win).astype(BF16), w) * ps).astype(o_ref.dtype)
            o_ref[tc:seq, :] = (_dot(segment(lat0, t_lat, win).astype(BF16), w) * ps).astype(o_ref.dtype)


def _pool(z_pool, pool_w, pool_scale, n_batch, seq, tc):
    n, bw = z_pool.shape
    ng = len(B_WINDOWS)
    gw = bw // ng
    pad_rows = seq + 4 * (max(B_WINDOWS) // 2)
    return pl.pallas_call(
        functools.partial(_pool_kernel, tc=tc),
        grid=(n_batch, ng),
        in_specs=[
            pl.BlockSpec((seq, gw), lambda b, g: (b, g)),
            pl.BlockSpec((1, gw, gw), lambda b, g: (g, 0, 0)),
            pl.BlockSpec((1, gw), lambda b, g: (0, g)),
        ],
        out_specs=pl.BlockSpec((seq, gw), lambda b, g: (b, g)),
        out_shape=jax.ShapeDtypeStruct((n, bw), BF16),
        scratch_shapes=[pltpu.VMEM((pad_rows, gw), F32)],
        compiler_params=_cparams(("parallel", "parallel")),
        name="pool_mixer",
    )(z_pool, pool_w, pool_scale.reshape(1, bw))


def _attn_kernel(q_ref, k_ref, v_ref, lam_ref, g_ref, o_ref, *, tc, lam_init):
    lq = lam_ref[...]
    lam = (jnp.exp(jnp.sum(lq[0:1] * lq[1:2], axis=-1, keepdims=True))
           - jnp.exp(jnp.sum(lq[2:3] * lq[3:4], axis=-1, keepdims=True)) + lam_init)
    lane = lax.broadcasted_iota(jnp.int32, (1, C_VHEAD), 1)
    q = q_ref[...]
    zero = jnp.zeros_like(q)
    q0 = jnp.where(lane < C_HEAD, q, zero)
    q1 = jnp.where(lane < C_HEAD, zero, q)
    g = g_ref[...] * (1.0 - lam_init)

    def attend(nk):
        k = k_ref[0:nk, :]

        def probs(qh):
            s = _dot_nt(qh, k)
            e = jnp.exp(s - jnp.max(s, axis=-1, keepdims=True))
            return e, 1.0 / jnp.sum(e, axis=-1, keepdims=True)

        e0, i0 = probs(q0)
        e1, i1 = probs(q1)
        amap = (e0 * i0 - e1 * (lam * i1)).astype(BF16)
        o = _dot(amap, v_ref[0:nk, :])
        o = o * lax.rsqrt(jnp.mean(o * o, axis=-1, keepdims=True) + 1e-5) * g
        o_ref[...] = o.astype(o_ref.dtype)

    @pl.when(pl.program_id(2) == 0)
    def _():
        attend(tc)

    @pl.when(pl.program_id(2) != 0)
    def _():
        attend(k_ref.shape[0])


def _attention(q, k, v, lam_qk, subln_g, lam_init, n_batch, seq, tc):
    n, w = q.shape
    nh = w // C_VHEAD
    nb = seq // RB
    kv = pl.BlockSpec((seq, C_VHEAD), lambda b, h, i: (b, h))
    qo = pl.BlockSpec((RB, C_VHEAD), lambda b, h, i: (b * nb + i, h))
    return pl.pallas_call(
        functools.partial(_attn_kernel, tc=tc, lam_init=lam_init),
        grid=(n_batch, nh, nb),
        in_specs=[
            qo, kv, kv,
            pl.BlockSpec(lam_qk.shape, lambda b, h, i: (0, 0)),
            pl.BlockSpec((1, C_VHEAD), lambda b, h, i: (0, 0)),
        ],
        out_specs=qo,
        out_shape=jax.ShapeDtypeStruct((n, w), BF16),
        compiler_params=_cparams(("parallel", "parallel", "parallel")),
        name="diff_attention",
    )(q, k, v, lam_qk, subln_g.reshape(1, C_VHEAD))


def _rope_tables(n_batch, t_lat, tc):
    n_freq = C_HEAD // 4
    inv = ROPE_BASE ** (-jnp.arange(n_freq, dtype=F32) / n_freq)
    rows = t_lat // GRID_W
    t_row = jnp.repeat(jnp.arange(rows, dtype=F32), GRID_W)
    t_col = jnp.tile(jnp.arange(GRID_W, dtype=F32), rows)
    ang_r = t_row[:, None] * inv
    ang_c = t_col[:, None] * inv
    cos = jnp.concatenate([jnp.cos(ang_r)] * 2 + [jnp.cos(ang_c)] * 2, axis=1)
    sin = jnp.concatenate([-jnp.sin(ang_r), jnp.sin(ang_r), -jnp.sin(ang_c), jnp.sin(ang_c)], axis=1)
    cos = jnp.concatenate([jnp.ones((tc, C_HEAD), F32), cos], axis=0)
    sin = jnp.concatenate([jnp.zeros((tc, C_HEAD), F32), sin], axis=0)
    cos = jnp.tile(jnp.tile(cos, (1, LANES // C_HEAD)), (n_batch, 1))
    sin = jnp.tile(jnp.tile(sin, (1, LANES // C_HEAD)), (n_batch, 1))
    return cos, sin


def kernel(x, c, ctx, c_ctx, w_ada, b_ada, norm_g, w_in, rkv_conv, decay_w0, decay_up, iclr_a0, iclr_up, gate_up, k_k, k_a, r_k, lnx_w, lnx_b, pool_w, pool_scale, lam_qk, subln_g, w_branch, w_out, w_ffn_in, w_ffn_out, final_g):
    n_batch, t_lat, d = x.shape
    tc = ctx.shape[1]
    depth = w_ada.shape[0]
    aw = k_k.shape[1]
    assert tc == RB and t_lat % RB == 0 and t_lat % GRID_W == 0 and aw % PAIR == 0
    seq = tc + t_lat
    nb = seq // RB
    n = n_batch * seq

    xa = jnp.concatenate([ctx, x], axis=1).reshape(n, d)
    mr = -(-(n_batch + 1) // SUBLANES) * SUBLANES
    c_all = jnp.zeros((mr, d), F32).at[:n_batch].set(c).at[n_batch].set(c_ctx)
    mods = _mod_tables(c_all, w_ada, b_ada)
    cos_t, sin_t = _rope_tables(n_batch, t_lat, tc)
    hid = lax.broadcasted_iota(jnp.int32, (aw, aw), 0) // A_HEAD
    e_head = (hid == hid.T).astype(BF16)

    o_small = 3 * aw
    o_pool = o_small + 2 * 2 * A_RANK + A_GATE_RANK
    o_q = o_pool + aw
    o_k = o_q + aw
    o_v = o_k + aw
    o_mix = o_v + aw

    for l in range(depth):
        lam_init = 0.8 - 0.6 * math.exp(-0.3 * l)
        mod = mods[l]
        wi = w_in[l].astype(BF16)

        h = _norm_mod(xa, norm_g[l, 0], mod, 0, 1, nb, n_batch)
        z_rkv = _mm(h, wi[:, 0:o_small], F32)
        z_small = _mm(h, wi[:, o_small:o_pool], F32)
        z_pool = _mm(h, wi[:, o_pool:o_q], F32)
        q = _mm_rope(h, wi[:, o_q:o_k], cos_t, sin_t, C_HEAD ** -0.5, BF16)
        k = _mm_rope(h, wi[:, o_k:o_v], cos_t, sin_t, 1.0, BF16)
        v = _mm(h, wi[:, o_v:o_mix], BF16)
        z_mix = _mm(h, wi[:, o_mix:], F32)

        r, va, nkk, kd, bd, lw = _rwkv_prepare(
            z_rkv, z_small, rkv_conv[l], decay_w0[l], decay_up[l], iclr_a0[l], iclr_up[l],
            k_k[l], k_a[l], e_head, seq, tc)
        y = _wkv(r, va, nkk, kd, bd, lw, n_batch, seq, tc)
        ya = _rwkv_output(y, r, va, kd, z_small, r_k[l], lnx_w[l], lnx_b[l], gate_up[l], e_head)
        yb = _pool(z_pool, pool_w[l].astype(BF16), pool_scale[l], n_batch, seq, tc)
        yc = _attention(q, k, v, lam_qk[l], subln_g[l], lam_init, n_batch, seq, tc)

        acc = _merge(ya, yb, yc, w_branch[l].astype(BF16), z_mix)
        xa = _mm_res(acc, w_out[l].astype(BF16), xa, mod, 2, nb, n_batch)

        h2 = _norm_mod(xa, norm_g[l, 1], mod, 3, 4, nb, n_batch)
        ff = _mm_swiglu(h2, w_ffn_in[l].astype(BF16), BF16)
        xa = _mm_res(ff, w_ffn_out[l].astype(BF16), xa, mod, 5, nb, n_batch)

    out = _final_norm(xa, final_g, nb, n_batch)
    return out.reshape(n_batch, t_lat, d)
```

```python
import functools
import math

import jax
import jax.numpy as jnp
from jax import lax
from jax.experimental import pallas as pl
from jax.experimental.pallas import tpu as pltpu

F32 = jnp.float32
BF16 = jnp.bfloat16

LANES = 128
SUBLANES = 8
MXU_COLS = 256
VMEM_LIMIT = 56 * 1024 * 1024

A_HEAD = 64
A_RANK = 64
A_GATE_RANK = 128
LNX_EPS = 64e-5
B_WINDOWS = (2, 4, 8, 16)
C_HEAD = 64
C_VHEAD = 2 * C_HEAD
ROPE_BASE = 10000.0
GRID_W = 64
N_MOD = 6
N_BRANCH = 3
RB = 256
CHUNK = 64
PAIR = 2 * A_HEAD
HALO = SUBLANES


def _cparams(sem):
    return pltpu.CompilerParams(dimension_semantics=sem, vmem_limit_bytes=VMEM_LIMIT)


def _pick(n, cands):
    for c in cands:
        if n % c == 0:
            return c
    raise ValueError(f"no tile for {n} in {cands}")


def _mod_row(blk, nb, n_batch):
    return jnp.where(blk % nb == 0, n_batch, blk // nb)


def _split3(x):
    hi = x.astype(BF16)
    r1 = x - hi.astype(F32)
    mid = r1.astype(BF16)
    lo = (r1 - mid.astype(F32)).astype(BF16)
    return hi, mid, lo


def _dot(a, b):
    return jnp.dot(a, b, preferred_element_type=F32)


def _dot_nt(a, b):
    return lax.dot_general(a, b, (((1,), (1,)), ((), ())), preferred_element_type=F32)


def _dot_tn(a, b):
    return lax.dot_general(a, b, (((0,), (0,)), ((), ())), preferred_element_type=F32)


def _head_sum(x, e):
    hi = x.astype(BF16)
    lo = (x - hi.astype(F32)).astype(BF16)
    w = e.shape[0]
    return jnp.concatenate(
        [_dot(hi[:, s:s + w], e) + _dot(lo[:, s:s + w], e) for s in range(0, x.shape[1], w)], axis=1)


def _dot_e3(e, x):
    hi, mid, lo = _split3(x)
    return _dot(e, hi) + _dot(e, mid) + _dot(e, lo)


def _dot_hl(x, w):
    xh = x.astype(BF16)
    xl = (x - xh.astype(F32)).astype(BF16)
    wh = w.astype(BF16)
    wl = (w - wh.astype(F32)).astype(BF16)
    return _dot(xh, wh) + _dot(xh, wl) + _dot(xl, wh)


def _mod_kernel(c_ref, w_ref, b_ref, o_ref):
    c = c_ref[...]
    s = (c * jax.nn.sigmoid(c)).astype(BF16)
    o_ref[0] = _dot(s, w_ref[0].astype(BF16)) + b_ref[0]


def _mod_tables(c_all, w_ada, b_ada):
    nl, d, n6 = w_ada.shape
    mr = c_all.shape[0]
    tn = _pick(n6, (1024, 512, 256, 128))
    return pl.pallas_call(
        _mod_kernel,
        grid=(nl, n6 // tn),
        in_specs=[
            pl.BlockSpec((mr, d), lambda l, j: (0, 0)),
            pl.BlockSpec((1, d, tn), lambda l, j: (l, 0, j)),
            pl.BlockSpec((1, 1, tn), lambda l, j: (l, 0, j)),
        ],
        out_specs=pl.BlockSpec((1, mr, tn), lambda l, j: (l, 0, j)),
        out_shape=jax.ShapeDtypeStruct((nl, mr, n6), F32),
        compiler_params=_cparams(("parallel", "parallel")),
        name="adaln_table",
    )(c_all, w_ada, b_ada.reshape(nl, 1, n6))


def _norm_kernel(x_ref, g_ref, shift_ref, scale_ref, o_ref, *, nb, n_batch):
    row = _mod_row(pl.program_id(0), nb, n_batch)
    x = x_ref[...]
    y = x * lax.rsqrt(jnp.mean(x * x, axis=-1, keepdims=True) + 1e-6) * g_ref[...]
    y = y * (1.0 + scale_ref[pl.ds(row, 1), :]) + shift_ref[pl.ds(row, 1), :]
    o_ref[...] = y.astype(o_ref.dtype)


def _norm_mod(x, g, mod, shift_idx, scale_idx, nb, n_batch):
    n, d = x.shape
    mr = mod.shape[0]
    return pl.pallas_call(
        functools.partial(_norm_kernel, nb=nb, n_batch=n_batch),
        grid=(n // RB,),
        in_specs=[
            pl.BlockSpec((RB, d), lambda i: (i, 0)),
            pl.BlockSpec((1, d), lambda i: (0, 0)),
            pl.BlockSpec((mr, d), lambda i: (0, shift_idx)),
            pl.BlockSpec((mr, d), lambda i: (0, scale_idx)),
        ],
        out_specs=pl.BlockSpec((RB, d), lambda i: (i, 0)),
        out_shape=jax.ShapeDtypeStruct((n, d), BF16),
        compiler_params=_cparams(("parallel",)),
        name="norm_mod",
    )(x, g.reshape(1, d), mod, mod)


def _final_norm_kernel(x_ref, g_ref, o_ref):
    x = x_ref[...]
    o_ref[...] = x * lax.rsqrt(jnp.mean(x * x, axis=-1, keepdims=True) + 1e-6) * g_ref[...]


def _final_norm(x, g, nb, n_batch):
    n, d = x.shape
    nlb = nb - 1
    return pl.pallas_call(
        _final_norm_kernel,
        grid=(n_batch, nlb),
        in_specs=[
            pl.BlockSpec((RB, d), lambda b, i: (b * nb + i + 1, 0)),
            pl.BlockSpec((1, d), lambda b, i: (0, 0)),
        ],
        out_specs=pl.BlockSpec((RB, d), lambda b, i: (b * nlb + i, 0)),
        out_shape=jax.ShapeDtypeStruct((n_batch * nlb * RB, d), F32),
        compiler_params=_cparams(("parallel", "parallel")),
        name="final_norm",
    )(x, g.reshape(1, d))


def _mm_plain_kernel(a_ref, w_ref, o_ref):
    o_ref[...] = _dot(a_ref[...], w_ref[...]).astype(o_ref.dtype)


def _mm_tiles(n, k, m):
    tm = _pick(n, (1024, 512, 256)) if k <= 2048 else _pick(n, (512, 256))
    tn = _pick(m, (512, 384, 256, 128))
    return tm, tn


def _mm(a, w, out_dtype):
    n, k = a.shape
    m = w.shape[1]
    tm, tn = _mm_tiles(n, k, m)
    return pl.pallas_call(
        _mm_plain_kernel,
        grid=(n // tm, m // tn),
        in_specs=[
            pl.BlockSpec((tm, k), lambda i, j: (i, 0)),
            pl.BlockSpec((k, tn), lambda i, j: (0, j)),
        ],
        out_specs=pl.BlockSpec((tm, tn), lambda i, j: (i, j)),
        out_shape=jax.ShapeDtypeStruct((n, m), out_dtype),
        compiler_params=_cparams(("parallel", "parallel")),
        name="mm_plain",
    )(a, w)


def _mm_rope_kernel(a_ref, w_ref, cos_ref, sin_ref, o_ref, *, scale):
    acc = _dot(a_ref[...], w_ref[...])
    lane = lax.broadcasted_iota(jnp.int32, (1, LANES), 1)
    first_half = (lane % (C_HEAD // 2)) < (C_HEAD // 4)
    cos = cos_ref[...]
    sin = sin_ref[...]
    for s in range(acc.shape[1] // LANES):
        u = acc[:, s * LANES:(s + 1) * LANES]
        partner = jnp.where(first_half,
                            pltpu.roll(u, LANES - C_HEAD // 4, 1),
                            pltpu.roll(u, C_HEAD // 4, 1))
        o_ref[:, s * LANES:(s + 1) * LANES] = ((u * cos + partner * sin) * scale).astype(o_ref.dtype)


def _mm_rope(a, w, cos_t, sin_t, scale, out_dtype):
    n, k = a.shape
    m = w.shape[1]
    tm, tn = _mm_tiles(n, k, m)
    return pl.pallas_call(
        functools.partial(_mm_rope_kernel, scale=scale),
        grid=(n // tm, m // tn),
        in_specs=[
            pl.BlockSpec((tm, k), lambda i, j: (i, 0)),
            pl.BlockSpec((k, tn), lambda i, j: (0, j)),
            pl.BlockSpec((tm, LANES), lambda i, j: (i, 0)),
            pl.BlockSpec((tm, LANES), lambda i, j: (i, 0)),
        ],
        out_specs=pl.BlockSpec((tm, tn), lambda i, j: (i, j)),
        out_shape=jax.ShapeDtypeStruct((n, m), out_dtype),
        compiler_params=_cparams(("parallel", "parallel")),
        name="mm_rope",
    )(a, w, cos_t, sin_t)


def _mm_swiglu_kernel(a_ref, wg_ref, wu_ref, o_ref):
    a = a_ref[...]
    g = _dot(a, wg_ref[...])
    u = _dot(a, wu_ref[...])
    o_ref[...] = (g * jax.nn.sigmoid(g) * u).astype(o_ref.dtype)


def _mm_swiglu(a, w, out_dtype):
    n, k = a.shape
    f = w.shape[1] // 2
    tm, tn = _mm_tiles(n, k, f)
    nj = f // tn
    return pl.pallas_call(
        _mm_swiglu_kernel,
        grid=(n // tm, nj),
        in_specs=[
            pl.BlockSpec((tm, k), lambda i, j: (i, 0)),
            pl.BlockSpec((k, tn), lambda i, j: (0, j)),
            pl.BlockSpec((k, tn), lambda i, j: (0, j + nj)),
        ],
        out_specs=pl.BlockSpec((tm, tn), lambda i, j: (i, j)),
        out_shape=jax.ShapeDtypeStruct((n, f), out_dtype),
        compiler_params=_cparams(("parallel", "parallel")),
        name="mm_swiglu",
    )(a, w, w)


def _mm_res_kernel(a_ref, w_ref, res_ref, gate_ref, o_ref, *, nb, n_batch):
    acc = _dot(a_ref[...], w_ref[...])
    tm = acc.shape[0]
    for s in range(tm // RB):
        row = _mod_row(pl.program_id(0) * (tm // RB) + s, nb, n_batch)
        gate = gate_ref[pl.ds(row, 1), :]
        rows = slice(s * RB, (s + 1) * RB)
        o_ref[rows, :] = res_ref[rows, :] + gate * acc[rows, :]


def _mm_res(a, w, res, mod, gate_idx, nb, n_batch):
    n, k = a.shape
    m = w.shape[1]
    mr = mod.shape[0]
    tm, tn = _mm_tiles(n, k, m)
    goff = gate_idx * (m // tn)
    return pl.pallas_call(
        functools.partial(_mm_res_kernel, nb=nb, n_batch=n_batch),
        grid=(n // tm, m // tn),
        in_specs=[
            pl.BlockSpec((tm, k), lambda i, j: (i, 0)),
            pl.BlockSpec((k, tn), lambda i, j: (0, j)),
            pl.BlockSpec((tm, tn), lambda i, j: (i, j)),
            pl.BlockSpec((mr, tn), lambda i, j: (0, goff + j)),
        ],
        out_specs=pl.BlockSpec((tm, tn), lambda i, j: (i, j)),
        out_shape=jax.ShapeDtypeStruct((n, m), F32),
        compiler_params=_cparams(("parallel", "parallel")),
        name="mm_residual",
    )(a, w, res, mod)


def _merge_kernel(ya_ref, yb_ref, yc_ref, w_ref, za_ref, zb_ref, zc_ref, o_ref):
    acc = jax.nn.sigmoid(za_ref[...]) * _dot(ya_ref[...], w_ref[0])
    acc += jax.nn.sigmoid(zb_ref[...]) * _dot(yb_ref[...], w_ref[1])
    acc += jax.nn.sigmoid(zc_ref[...]) * _dot(yc_ref[...], w_ref[2])
    o_ref[...] = acc.astype(o_ref.dtype)


def _merge(ya, yb, yc, w_branch, z_mix):
    n, kw = ya.shape
    d = w_branch.shape[2]
    tm = _pick(n, (1024, 512, 256))
    tn = _pick(d, (512, 256, 128))
    nj = d // tn
    yspec = pl.BlockSpec((tm, kw), lambda i, j: (i, 0))
    return pl.pallas_call(
        _merge_kernel,
        grid=(n // tm, nj),
        in_specs=[
            yspec, yspec, yspec,
            pl.BlockSpec((N_BRANCH, kw, tn), lambda i, j: (0, 0, j)),
            pl.BlockSpec((tm, tn), lambda i, j: (i, j)),
            pl.BlockSpec((tm, tn), lambda i, j: (i, nj + j)),
            pl.BlockSpec((tm, tn), lambda i, j: (i, 2 * nj + j)),
        ],
        out_specs=pl.BlockSpec((tm, tn), lambda i, j: (i, j)),
        out_shape=jax.ShapeDtypeStruct((n, d), BF16),
        compiler_params=_cparams(("parallel", "parallel")),
        name="merge",
    )(ya, yb, yc, w_branch, z_mix, z_mix, z_mix)


def _prep_kernel(zr_ref, zp_ref, zn_ref, zs_ref, conv_ref, w0_ref, wup_ref, a0_ref, aup_ref,
                 kk_ref, ka_ref, e_ref,
                 r_ref, v_ref, nkk_ref, kd_ref, bd_ref, lw_ref, pad_ref, *, seq, tc):
    rt = zr_ref.shape[0]
    aw = r_ref.shape[1]
    pos = (pl.program_id(0) * rt) % seq
    prev_ok = jnp.logical_and(pos != 0, pos != tc)
    end = pos + rt
    next_ok = jnp.logical_and(end != seq, end != tc)
    pad_ref[0:HALO, :] = jnp.where(prev_ok, zp_ref[...], 0.0)
    pad_ref[HALO:HALO + rt, :] = zr_ref[...]
    pad_ref[HALO + rt:HALO + rt + HALO, :] = jnp.where(next_ok, zn_ref[...], 0.0)
    c = (pad_ref[HALO - 1:HALO - 1 + rt, :] * conv_ref[0:1, :]
         + pad_ref[HALO:HALO + rt, :] * conv_ref[1:2, :]
         + pad_ref[HALO + 1:HALO + 1 + rt, :] * conv_ref[2:3, :])
    r = c[:, 0:aw]
    k = c[:, aw:2 * aw]
    v = c[:, 2 * aw:3 * aw]
    r_ref[...] = r
    v_ref[...] = v
    kx = k * kk_ref[...]
    norm = jnp.sqrt(_head_sum(kx * kx, e_ref[...]))
    kk = kx / jnp.maximum(norm, 1e-12)
    nkk_ref[...] = -kk
    zs = zs_ref[...]
    ka = ka_ref[...]
    for d in range(2):
        zd = zs[:, d * A_RANK:(d + 1) * A_RANK]
        w_raw = w0_ref[d:d + 1, :] + _dot_hl(jnp.tanh(zd), wup_ref[d])
        lw_ref[d] = -math.exp(-0.5) * jax.nn.sigmoid(w_raw)
        za = zs[:, 2 * A_RANK + d * A_RANK:2 * A_RANK + (d + 1) * A_RANK]
        a = jax.nn.sigmoid(a0_ref[d:d + 1, :] + _dot_hl(za, aup_ref[d]))
        kd_ref[d] = k * (1.0 + (a - 1.0) * ka)
        bd_ref[d] = kk * a


def _rwkv_prepare(z_rkv, z_small, conv_w, w0, w_up, a0, a_up, k_k, k_a, e_head, seq, tc):
    n, w3 = z_rkv.shape
    aw = w3 // 3
    rt = 128
    nh = rt // HALO
    last = n // HALO - 1
    full2 = lambda i: (0, 0)
    full3 = lambda i: (0, 0, 0)
    row = jax.ShapeDtypeStruct((n, aw), F32)
    drow = jax.ShapeDtypeStruct((2, n, aw), F32)
    return pl.pallas_call(
        functools.partial(_prep_kernel, seq=seq, tc=tc),
        grid=(n // rt,),
        in_specs=[
            pl.BlockSpec((rt, w3), lambda i: (i, 0)),
            pl.BlockSpec((HALO, w3), lambda i: (jnp.maximum(i * nh - 1, 0), 0)),
            pl.BlockSpec((HALO, w3), lambda i: (jnp.minimum((i + 1) * nh, last), 0)),
            pl.BlockSpec((rt, z_small.shape[1]), lambda i: (i, 0)),
            pl.BlockSpec(conv_w.shape, full2),
            pl.BlockSpec(w0.shape, full2),
            pl.BlockSpec(w_up.shape, full3),
            pl.BlockSpec(a0.shape, full2),
            pl.BlockSpec(a_up.shape, full3),
            pl.BlockSpec((1, aw), full2),
            pl.BlockSpec((1, aw), full2),
            pl.BlockSpec(e_head.shape, full2),
        ],
        out_specs=[
            pl.BlockSpec((rt, aw), lambda i: (i, 0)),
            pl.BlockSpec((rt, aw), lambda i: (i, 0)),
            pl.BlockSpec((rt, aw), lambda i: (i, 0)),
            pl.BlockSpec((2, rt, aw), lambda i: (0, i, 0)),
            pl.BlockSpec((2, rt, aw), lambda i: (0, i, 0)),
            pl.BlockSpec((2, rt, aw), lambda i: (0, i, 0)),
        ],
        out_shape=[row, row, row, drow, drow, drow],
        scratch_shapes=[pltpu.VMEM((rt + 2 * HALO, w3), F32)],
        compiler_params=_cparams(("parallel",)),
        name="rwkv_prepare",
    )(z_rkv, z_rkv, z_rkv, z_small, conv_w, w0, w_up, a0, a_up,
      k_k.reshape(1, aw), k_a.reshape(1, aw), e_head)


def _wkv_kernel(r_ref, v_ref, a_ref, k_ref, b_ref, lw_ref, y_ref, s_ref):
    d = pl.program_id(1)
    fwd = d == 0
    c = r_ref.shape[0]
    c2 = 2 * c
    n_pair = r_ref.shape[1] // PAIR

    @pl.when(pl.program_id(2) == 0)
    def _():
        s_ref[...] = jnp.zeros_like(s_ref)

    sgn = jnp.where(fwd, 1, -1)
    ti = lax.broadcasted_iota(jnp.int32, (c, c), 0)
    si = lax.broadcasted_iota(jnp.int32, (c, c), 1)
    cum = jnp.where((si - ti) * sgn <= 0, 1.0, 0.0).astype(BF16)
    cs_all = _dot_e3(cum, lw_ref[...])
    row_c = lax.broadcasted_iota(jnp.int32, (c, 1), 0)
    last_row = row_c == jnp.where(fwd, c - 1, 0)

    i2 = lax.broadcasted_iota(jnp.int32, (c2, c2), 0)
    j2 = lax.broadcasted_iota(jnp.int32, (c2, c2), 1)
    order = (j2 % c - i2 % c) * sgn
    strict = order < 0
    incl = order <= 0
    xr = i2 ^ j2
    eye = (i2 == j2).astype(F32)
    lane = lax.broadcasted_iota(jnp.int32, (1, PAIR), 1)
    m0 = (lane < A_HEAD).astype(F32)
    m1 = 1.0 - m0

    def stack(x):
        return jnp.concatenate([x * m0, x * m1], axis=0)

    pairs = range(n_pair)
    cols = [slice(p * PAIR, (p + 1) * PAIR) for p in pairs]
    zl, zr, v_s, bk_end, dec_end = [], [], [], [], []
    for p in pairs:
        lw = lw_ref[:, cols[p]]
        cs = cs_all[:, cols[p]]
        cs_end = jnp.sum(jnp.where(last_row, cs, 0.0), axis=0, keepdims=True)
        g_inv = jnp.exp(-cs)
        g_end = jnp.exp(cs_end - cs)
        kd = k_ref[:, cols[p]]
        bd = b_ref[:, cols[p]]
        a_t = stack(a_ref[:, cols[p]] * jnp.exp(cs - lw))
        r_t = stack(r_ref[:, cols[p]] * jnp.exp(cs))
        zl.append(jnp.concatenate([a_t, r_t], axis=0).astype(BF16))
        zr.append(jnp.concatenate([stack(bd * g_inv), stack(kd * g_inv)], axis=0).astype(BF16))
        v_s.append(stack(v_ref[:, cols[p]]).astype(BF16))
        bk_end.append(jnp.concatenate([stack(bd * g_end), stack(kd * g_end)], axis=0).astype(BF16))
        dec_end.append(jnp.exp(cs_end))

    aa = [_dot_nt(zl[p], zr[p]) for p in pairs]
    s0 = [s_ref[p] for p in pairs]
    zs = [_dot_nt(zl[p], s0[p].astype(BF16)) for p in pairs]
    a_ab = [jnp.where(strict, aa[p][0:c2, 0:c2], 0.0) for p in pairs]
    av = [_dot(jnp.concatenate([jnp.where(strict, aa[p][0:c2, c2:2 * c2], 0.0),
                                jnp.where(incl, aa[p][c2:2 * c2, c2:2 * c2], 0.0)], axis=0).astype(BF16),
               v_s[p]) for p in pairs]
    a_rb = [jnp.where(incl, aa[p][c2:2 * c2, 0:c2], 0.0).astype(BF16) for p in pairs]

    n1 = [jnp.where(xr < 8, a_ab[p], 0.0).astype(BF16) for p in pairs]
    n2 = [_dot(n1[p], n1[p]).astype(BF16) for p in pairs]
    t = [eye + n1[p].astype(F32) for p in pairs]
    t = [t[p] + _dot(t[p].astype(BF16), n2[p]) for p in pairs]
    n4 = [_dot(n2[p], n2[p]).astype(BF16) for p in pairs]
    t = [t[p] + _dot(t[p].astype(BF16), n4[p]) for p in pairs]
    m = 8
    while m < c:
        level = jnp.logical_and(xr >= m, xr < 2 * m)
        tb = [t[p].astype(BF16) for p in pairs]
        x = [_dot(jnp.where(level, a_ab[p], 0.0).astype(BF16), tb[p]).astype(BF16) for p in pairs]
        t = [t[p] + _dot(tb[p], x[p]) for p in pairs]
        m *= 2

    u = [_dot(t[p].astype(BF16), (zs[p][0:c2] + av[p][0:c2]).astype(BF16)).astype(BF16) for p in pairs]
    yu = [_dot(a_rb[p], u[p]) for p in pairs]
    ds = [_dot_tn(jnp.concatenate([u[p], v_s[p]], axis=0), bk_end[p]) for p in pairs]
    for p in pairs:
        y_s = zs[p][c2:2 * c2] + av[p][c2:2 * c2] + yu[p]
        y_ref[:, cols[p]] = y_s[0:c] + y_s[c:c2]
        s_ref[p] = s0[p] * dec_end[p] + ds[p]


def _wkv(r, v, nkk, kd, bd, lw, n_batch, seq, tc):
    n, aw = r.shape
    c = CHUNK
    nch = seq // c
    ncc = tc // c

    def blk(b, d, i):
        back = jnp.where(i < ncc, ncc - 1 - i, nch - 1 - (i - ncc))
        return b * nch + jnp.where(d == 0, i, back)

    shared = pl.BlockSpec((c, aw), lambda b, d, i: (blk(b, d, i), 0))
    perdir = pl.BlockSpec((None, c, aw), lambda b, d, i: (d, blk(b, d, i), 0))
    return pl.pallas_call(
        _wkv_kernel,
        grid=(n_batch, 2, nch),
        in_specs=[shared, shared, shared, perdir, perdir, perdir],
        out_specs=perdir,
        out_shape=jax.ShapeDtypeStruct((2, n, aw), F32),
        scratch_shapes=[pltpu.VMEM((aw // PAIR, PAIR, PAIR), F32)],
        compiler_params=_cparams(("parallel", "parallel", "arbitrary")),
        name="wkv_scan",
    )(r, v, nkk, kd, bd, lw)


def _rwkv_out_kernel(y_ref, r_ref, v_ref, kd_ref, zs_ref, rk_ref, lw_ref, lb_ref, gup_ref, e_ref, o_ref):
    e = e_ref[...]
    inv = 1.0 / A_HEAD
    y = y_ref[0] + y_ref[1]
    mu = _head_sum(y, e) * inv
    yc = y - mu
    var = _head_sum(yc * yc, e) * inv
    yh = yc * lax.rsqrt(var + LNX_EPS) * lw_ref[...] + lb_ref[...]
    bonus = _head_sum(r_ref[...] * (kd_ref[0] + kd_ref[1]) * rk_ref[...], e) * v_ref[...]
    zg = zs_ref[...][:, 2 * 2 * A_RANK:2 * 2 * A_RANK + A_GATE_RANK]
    g = _dot_hl(jax.nn.sigmoid(zg), gup_ref[...])
    o_ref[...] = ((yh + bonus) * g).astype(o_ref.dtype)


def _rwkv_output(y, r, v, kd, z_small, r_k, lnx_w, lnx_b, gate_up, e_head):
    n, aw = r.shape
    rt = RB
    rowspec = pl.BlockSpec((rt, aw), lambda i: (i, 0))
    dspec = pl.BlockSpec((2, rt, aw), lambda i: (0, i, 0))
    vec = pl.BlockSpec((1, aw), lambda i: (0, 0))
    return pl.pallas_call(
        _rwkv_out_kernel,
        grid=(n // rt,),
        in_specs=[
            dspec, rowspec, rowspec, dspec,
            pl.BlockSpec((rt, z_small.shape[1]), lambda i: (i, 0)),
            vec, vec, vec,
            pl.BlockSpec(gate_up.shape, lambda i: (0, 0)),
            pl.BlockSpec(e_head.shape, lambda i: (0, 0)),
        ],
        out_specs=rowspec,
        out_shape=jax.ShapeDtypeStruct((n, aw), BF16),
        compiler_params=_cparams(("parallel",)),
        name="rwkv_output",
    )(y, r, v, kd, z_small, r_k.reshape(1, aw), lnx_w.reshape(1, aw), lnx_b.reshape(1, aw),
      gate_up, e_head)


def _pool_kernel(u_ref, w_ref, ps_ref, o_ref, pad_ref, *, tc):
    seq = u_ref.shape[0]
    t_lat = seq - tc
    half_max = max(B_WINDOWS) // 2
    lat0 = half_max + tc + 2 * half_max
    gw = u_ref.shape[1]
    zeros = jnp.zeros((2 * half_max, gw), F32)
    pad_ref[0:half_max, :] = zeros[0:half_max]
    pad_ref[half_max:half_max + tc, :] = u_ref[0:tc, :]
    pad_ref[half_max + tc:lat0, :] = zeros
    pad_ref[lat0:lat0 + t_lat, :] = u_ref[tc:seq, :]
    pad_ref[lat0 + t_lat:lat0 + t_lat + half_max, :] = zeros[0:half_max]

    def segment(start, n, win):
        half = win // 2
        acc = pad_ref[start - half:start - half + n, :]
        for j in range(1 - half, half):
            acc = acc + pad_ref[start + j:start + j + n, :]
        t = lax.broadcasted_iota(jnp.int32, (n, gw), 0)
        cnt = jnp.minimum(t + half, n) - jnp.maximum(t - half, 0)
        return acc / cnt.astype(F32) - pad_ref[start:start + n, :]

    for gi, win in enumerate(B_WINDOWS):
        @pl.when(pl.program_id(1) == gi)
        def _(win=win):
            w = w_ref[0]
            ps = ps_ref[...]
            o_ref[0:tc, :] = (_dot(segment(half_max, tc, win).astype(BF16), w) * ps).astype(o_ref.dtype)
            o_ref[tc:seq, :] = (_dot(segment(lat0, t_lat, win).astype(BF16), w) * ps).astype(o_ref.dtype)


def _pool(z_pool, pool_w, pool_scale, n_batch, seq, tc):
    n, bw = z_pool.shape
    ng = len(B_WINDOWS)
    gw = bw // ng
    pad_rows = seq + 4 * (max(B_WINDOWS) // 2)
    return pl.pallas_call(
        functools.partial(_pool_kernel, tc=tc),
        grid=(n_batch, ng),
        in_specs=[
            pl.BlockSpec((seq, gw), lambda b, g: (b, g)),
            pl.BlockSpec((1, gw, gw), lambda b, g: (g, 0, 0)),
            pl.BlockSpec((1, gw), lambda b, g: (0, g)),
        ],
        out_specs=pl.BlockSpec((seq, gw), lambda b, g: (b, g)),
        out_shape=jax.ShapeDtypeStruct((n, bw), BF16),
        scratch_shapes=[pltpu.VMEM((pad_rows, gw), F32)],
        compiler_params=_cparams(("parallel", "parallel")),
        name="pool_mixer",
    )(z_pool, pool_w, pool_scale.reshape(1, bw))


def _attn_kernel(q_ref, k_ref, v_ref, lam_ref, g_ref, o_ref, *, tc, lam_init):
    lq = lam_ref[...]
    lam = (jnp.exp(jnp.sum(lq[0:1] * lq[1:2], axis=-1, keepdims=True))
           - jnp.exp(jnp.sum(lq[2:3] * lq[3:4], axis=-1, keepdims=True)) + lam_init)
    lane = lax.broadcasted_iota(jnp.int32, (1, C_VHEAD), 1)
    q = q_ref[...]
    zero = jnp.zeros_like(q)
    q0 = jnp.where(lane < C_HEAD, q, zero)
    q1 = jnp.where(lane < C_HEAD, zero, q)
    g = g_ref[...] * (1.0 - lam_init)

    def attend(nk):
        k = k_ref[0:nk, :]

        def probs(qh):
            s = _dot_nt(qh, k)
            e = jnp.exp(s - jnp.max(s, axis=-1, keepdims=True))
            return e, 1.0 / jnp.sum(e, axis=-1, keepdims=True)

        e0, i0 = probs(q0)
        e1, i1 = probs(q1)
        amap = (e0 * i0 - e1 * (lam * i1)).astype(BF16)
        o = _dot(amap, v_ref[0:nk, :])
        o = o * lax.rsqrt(jnp.mean(o * o, axis=-1, keepdims=True) + 1e-5) * g
        o_ref[...] = o.astype(o_ref.dtype)

    @pl.when(pl.program_id(2) == 0)
    def _():
        attend(tc)

    @pl.when(pl.program_id(2) != 0)
    def _():
        attend(k_ref.shape[0])


def _attention(q, k, v, lam_qk, subln_g, lam_init, n_batch, seq, tc):
    n, w = q.shape
    nh = w // C_VHEAD
    nb = seq // RB
    kv = pl.BlockSpec((seq, C_VHEAD), lambda b, h, i: (b, h))
    qo = pl.BlockSpec((RB, C_VHEAD), lambda b, h, i: (b * nb + i, h))
    return pl.pallas_call(
        functools.partial(_attn_kernel, tc=tc, lam_init=lam_init),
        grid=(n_batch, nh, nb),
        in_specs=[
            qo, kv, kv,
            pl.BlockSpec(lam_qk.shape, lambda b, h, i: (0, 0)),
            pl.BlockSpec((1, C_VHEAD), lambda b, h, i: (0, 0)),
        ],
        out_specs=qo,
        out_shape=jax.ShapeDtypeStruct((n, w), BF16),
        compiler_params=_cparams(("parallel", "parallel", "parallel")),
        name="diff_attention",
    )(q, k, v, lam_qk, subln_g.reshape(1, C_VHEAD))


def _rope_tables(n_batch, t_lat, tc):
    n_freq = C_HEAD // 4
    inv = ROPE_BASE ** (-jnp.arange(n_freq, dtype=F32) / n_freq)
    rows = t_lat // GRID_W
    t_row = jnp.repeat(jnp.arange(rows, dtype=F32), GRID_W)
    t_col = jnp.tile(jnp.arange(GRID_W, dtype=F32), rows)
    ang_r = t_row[:, None] * inv
    ang_c = t_col[:, None] * inv
    cos = jnp.concatenate([jnp.cos(ang_r)] * 2 + [jnp.cos(ang_c)] * 2, axis=1)
    sin = jnp.concatenate([-jnp.sin(ang_r), jnp.sin(ang_r), -jnp.sin(ang_c), jnp.sin(ang_c)], axis=1)
    cos = jnp.concatenate([jnp.ones((tc, C_HEAD), F32), cos], axis=0)
    sin = jnp.concatenate([jnp.zeros((tc, C_HEAD), F32), sin], axis=0)
    cos = jnp.tile(jnp.tile(cos, (1, LANES // C_HEAD)), (n_batch, 1))
    sin = jnp.tile(jnp.tile(sin, (1, LANES // C_HEAD)), (n_batch, 1))
    return cos, sin


def kernel(x, c, ctx, c_ctx, w_ada, b_ada, norm_g, w_in, rkv_conv, decay_w0, decay_up, iclr_a0, iclr_up, gate_up, k_k, k_a, r_k, lnx_w, lnx_b, pool_w, pool_scale, lam_qk, subln_g, w_branch, w_out, w_ffn_in, w_ffn_out, final_g):
    n_batch, t_lat, d = x.shape
    tc = ctx.shape[1]
    depth = w_ada.shape[0]
    aw = k_k.shape[1]
    assert tc == RB and t_lat % RB == 0 and t_lat % GRID_W == 0 and aw % PAIR == 0
    seq = tc + t_lat
    nb = seq // RB
    n = n_batch * seq

    xa = jnp.concatenate([ctx, x], axis=1).reshape(n, d)
    mr = -(-(n_batch + 1) // SUBLANES) * SUBLANES
    c_all = jnp.zeros((mr, d), F32).at[:n_batch].set(c).at[n_batch].set(c_ctx)
    mods = _mod_tables(c_all, w_ada, b_ada)
    cos_t, sin_t = _rope_tables(n_batch, t_lat, tc)
    hid = lax.broadcasted_iota(jnp.int32, (MXU_COLS, MXU_COLS), 0) // A_HEAD
    e_head = (hid == hid.T).astype(BF16)

    o_small = 3 * aw
    o_pool = o_small + 2 * 2 * A_RANK + A_GATE_RANK
    o_q = o_pool + aw
    o_k = o_q + aw
    o_v = o_k + aw
    o_mix = o_v + aw

    for l in range(depth):
        lam_init = 0.8 - 0.6 * math.exp(-0.3 * l)
        mod = mods[l]
        wi = w_in[l].astype(BF16)

        h = _norm_mod(xa, norm_g[l, 0], mod, 0, 1, nb, n_batch)
        z_rkv = _mm(h, wi[:, 0:o_small], F32)
        z_small = _mm(h, wi[:, o_small:o_pool], F32)
        z_pool = _mm(h, wi[:, o_pool:o_q], F32)
        q = _mm_rope(h, wi[:, o_q:o_k], cos_t, sin_t, C_HEAD ** -0.5, BF16)
        k = _mm_rope(h, wi[:, o_k:o_v], cos_t, sin_t, 1.0, BF16)
        v = _mm(h, wi[:, o_v:o_mix], BF16)
        z_mix = _mm(h, wi[:, o_mix:], F32)

        r, va, nkk, kd, bd, lw = _rwkv_prepare(
            z_rkv, z_small, rkv_conv[l], decay_w0[l], decay_up[l], iclr_a0[l], iclr_up[l],
            k_k[l], k_a[l], e_head, seq, tc)
        y = _wkv(r, va, nkk, kd, bd, lw, n_batch, seq, tc)
        ya = _rwkv_output(y, r, va, kd, z_small, r_k[l], lnx_w[l], lnx_b[l], gate_up[l], e_head)
        yb = _pool(z_pool, pool_w[l].astype(BF16), pool_scale[l], n_batch, seq, tc)
        yc = _attention(q, k, v, lam_qk[l], subln_g[l], lam_init, n_batch, seq, tc)

        acc = _merge(ya, yb, yc, w_branch[l].astype(BF16), z_mix)
        xa = _mm_res(acc, w_out[l].astype(BF16), xa, mod, 2, nb, n_batch)

        h2 = _norm_mod(xa, norm_g[l, 1], mod, 3, 4, nb, n_batch)
        ff = _mm_swiglu(h2, w_ffn_in[l].astype(BF16), BF16)
        xa = _mm_res(ff, w_ffn_out[l].astype(BF16), xa, mod, 5, nb, n_batch)

    out = _final_norm(xa, final_g, nb, n_batch)
    return out.reshape(n_batch, t_lat, d)
```

```python
import functools
import math

import jax
import jax.numpy as jnp
from jax import lax
from jax.experimental import pallas as pl
from jax.experimental.pallas import tpu as pltpu

F32 = jnp.float32
BF16 = jnp.bfloat16

LANES = 128
SUBLANES = 8
MXU_COLS = 256
VMEM_LIMIT = 56 * 1024 * 1024

A_HEAD = 64
A_RANK = 64
A_GATE_RANK = 128
LNX_EPS = 64e-5
B_WINDOWS = (2, 4, 8, 16)
C_HEAD = 64
C_VHEAD = 2 * C_HEAD
ROPE_BASE = 10000.0
GRID_W = 64
N_MOD = 6
N_BRANCH = 3
RB = 256
CHUNK = 64
PAIR = 2 * A_HEAD
HALO = SUBLANES


def _cparams(sem):
    return pltpu.CompilerParams(dimension_semantics=sem, vmem_limit_bytes=VMEM_LIMIT)


def _pick(n, cands):
    for c in cands:
        if n % c == 0:
            return c
    raise ValueError(f"no tile for {n} in {cands}")


def _mod_row(blk, nb, n_batch):
    return jnp.where(blk % nb == 0, n_batch, blk // nb)


def _split3(x):
    hi = x.astype(BF16)
    r1 = x - hi.astype(F32)
    mid = r1.astype(BF16)
    lo = (r1 - mid.astype(F32)).astype(BF16)
    return hi, mid, lo


def _dot(a, b):
    return jnp.dot(a, b, preferred_element_type=F32)


def _dot_nt(a, b):
    return lax.dot_general(a, b, (((1,), (1,)), ((), ())), preferred_element_type=F32)


def _dot_tn(a, b):
    return lax.dot_general(a, b, (((0,), (0,)), ((), ())), preferred_element_type=F32)


def _head_sum(x, e):
    hi = x.astype(BF16)
    lo = (x - hi.astype(F32)).astype(BF16)
    w = e.shape[0]
    return jnp.concatenate(
        [_dot(hi[:, s:s + w], e) + _dot(lo[:, s:s + w], e) for s in range(0, x.shape[1], w)], axis=1)


def _dot_e3(e, x):
    hi, mid, lo = _split3(x)
    return _dot(e, hi) + _dot(e, mid) + _dot(e, lo)


def _dot_hl(x, w):
    xh = x.astype(BF16)
    xl = (x - xh.astype(F32)).astype(BF16)
    wh = w.astype(BF16)
    wl = (w - wh.astype(F32)).astype(BF16)
    return _dot(xh, wh) + _dot(xh, wl) + _dot(xl, wh)


def _mod_kernel(c_ref, w_ref, b_ref, o_ref):
    c = c_ref[...]
    s = (c * jax.nn.sigmoid(c)).astype(BF16)
    o_ref[0] = _dot(s, w_ref[0].astype(BF16)) + b_ref[0]


def _mod_tables(c_all, w_ada, b_ada):
    nl, d, n6 = w_ada.shape
    mr = c_all.shape[0]
    tn = _pick(n6, (1024, 512, 256, 128))
    return pl.pallas_call(
        _mod_kernel,
        grid=(nl, n6 // tn),
        in_specs=[
            pl.BlockSpec((mr, d), lambda l, j: (0, 0)),
            pl.BlockSpec((1, d, tn), lambda l, j: (l, 0, j)),
            pl.BlockSpec((1, 1, tn), lambda l, j: (l, 0, j)),
        ],
        out_specs=pl.BlockSpec((1, mr, tn), lambda l, j: (l, 0, j)),
        out_shape=jax.ShapeDtypeStruct((nl, mr, n6), F32),
        compiler_params=_cparams(("parallel", "parallel")),
        name="adaln_table",
    )(c_all, w_ada, b_ada.reshape(nl, 1, n6))


def _norm_kernel(x_ref, g_ref, shift_ref, scale_ref, o_ref, *, nb, n_batch):
    row = _mod_row(pl.program_id(0), nb, n_batch)
    x = x_ref[...]
    y = x * lax.rsqrt(jnp.mean(x * x, axis=-1, keepdims=True) + 1e-6) * g_ref[...]
    y = y * (1.0 + scale_ref[pl.ds(row, 1), :]) + shift_ref[pl.ds(row, 1), :]
    o_ref[...] = y.astype(o_ref.dtype)


def _norm_mod(x, g, mod, shift_idx, scale_idx, nb, n_batch):
    n, d = x.shape
    mr = mod.shape[0]
    return pl.pallas_call(
        functools.partial(_norm_kernel, nb=nb, n_batch=n_batch),
        grid=(n // RB,),
        in_specs=[
            pl.BlockSpec((RB, d), lambda i: (i, 0)),
            pl.BlockSpec((1, d), lambda i: (0, 0)),
            pl.BlockSpec((mr, d), lambda i: (0, shift_idx)),
            pl.BlockSpec((mr, d), lambda i: (0, scale_idx)),
        ],
        out_specs=pl.BlockSpec((RB, d), lambda i: (i, 0)),
        out_shape=jax.ShapeDtypeStruct((n, d), BF16),
        compiler_params=_cparams(("parallel",)),
        name="norm_mod",
    )(x, g.reshape(1, d), mod, mod)


def _final_norm_kernel(x_ref, g_ref, o_ref):
    x = x_ref[...]
    o_ref[...] = x * lax.rsqrt(jnp.mean(x * x, axis=-1, keepdims=True) + 1e-6) * g_ref[...]


def _final_norm(x, g, nb, n_batch):
    n, d = x.shape
    nlb = nb - 1
    return pl.pallas_call(
        _final_norm_kernel,
        grid=(n_batch, nlb),
        in_specs=[
            pl.BlockSpec((RB, d), lambda b, i: (b * nb + i + 1, 0)),
            pl.BlockSpec((1, d), lambda b, i: (0, 0)),
        ],
        out_specs=pl.BlockSpec((RB, d), lambda b, i: (b * nlb + i, 0)),
        out_shape=jax.ShapeDtypeStruct((n_batch * nlb * RB, d), F32),
        compiler_params=_cparams(("parallel", "parallel")),
        name="final_norm",
    )(x, g.reshape(1, d))


def _mm_plain_kernel(a_ref, w_ref, o_ref):
    o_ref[...] = _dot(a_ref[...], w_ref[...]).astype(o_ref.dtype)


def _mm_tiles(n, k, m):
    tm = _pick(n, (1024, 512, 256)) if k <= 2048 else _pick(n, (512, 256))
    tn = _pick(m, (512, 384, 256, 128))
    return tm, tn


def _mm(a, w, out_dtype):
    n, k = a.shape
    m = w.shape[1]
    tm, tn = _mm_tiles(n, k, m)
    return pl.pallas_call(
        _mm_plain_kernel,
        grid=(n // tm, m // tn),
        in_specs=[
            pl.BlockSpec((tm, k), lambda i, j: (i, 0)),
            pl.BlockSpec((k, tn), lambda i, j: (0, j)),
        ],
        out_specs=pl.BlockSpec((tm, tn), lambda i, j: (i, j)),
        out_shape=jax.ShapeDtypeStruct((n, m), out_dtype),
        compiler_params=_cparams(("parallel", "parallel")),
        name="mm_plain",
    )(a, w)


def _mm_rope_kernel(a_ref, w_ref, cos_ref, sin_ref, o_ref, *, scale):
    acc = _dot(a_ref[...], w_ref[...])
    lane = lax.broadcasted_iota(jnp.int32, (1, LANES), 1)
    first_half = (lane % (C_HEAD // 2)) < (C_HEAD // 4)
    cos = cos_ref[...]
    sin = sin_ref[...]
    for s in range(acc.shape[1] // LANES):
        u = acc[:, s * LANES:(s + 1) * LANES]
        partner = jnp.where(first_half,
                            pltpu.roll(u, LANES - C_HEAD // 4, 1),
                            pltpu.roll(u, C_HEAD // 4, 1))
        o_ref[:, s * LANES:(s + 1) * LANES] = ((u * cos + partner * sin) * scale).astype(o_ref.dtype)


def _mm_rope(a, w, cos_t, sin_t, scale, out_dtype):
    n, k = a.shape
    m = w.shape[1]
    tm, tn = _mm_tiles(n, k, m)
    return pl.pallas_call(
        functools.partial(_mm_rope_kernel, scale=scale),
        grid=(n // tm, m // tn),
        in_specs=[
            pl.BlockSpec((tm, k), lambda i, j: (i, 0)),
            pl.BlockSpec((k, tn), lambda i, j: (0, j)),
            pl.BlockSpec((tm, LANES), lambda i, j: (i, 0)),
            pl.BlockSpec((tm, LANES), lambda i, j: (i, 0)),
        ],
        out_specs=pl.BlockSpec((tm, tn), lambda i, j: (i, j)),
        out_shape=jax.ShapeDtypeStruct((n, m), out_dtype),
        compiler_params=_cparams(("parallel", "parallel")),
        name="mm_rope",
    )(a, w, cos_t, sin_t)


def _mm_swiglu_kernel(a_ref, wg_ref, wu_ref, o_ref):
    a = a_ref[...]
    g = _dot(a, wg_ref[...])
    u = _dot(a, wu_ref[...])
    o_ref[...] = (g * jax.nn.sigmoid(g) * u).astype(o_ref.dtype)


def _mm_swiglu(a, w, out_dtype):
    n, k = a.shape
    f = w.shape[1] // 2
    tm, tn = _mm_tiles(n, k, f)
    nj = f // tn
    return pl.pallas_call(
        _mm_swiglu_kernel,
        grid=(n // tm, nj),
        in_specs=[
            pl.BlockSpec((tm, k), lambda i, j: (i, 0)),
            pl.BlockSpec((k, tn), lambda i, j: (0, j)),
            pl.BlockSpec((k, tn), lambda i, j: (0, j + nj)),
        ],
        out_specs=pl.BlockSpec((tm, tn), lambda i, j: (i, j)),
        out_shape=jax.ShapeDtypeStruct((n, f), out_dtype),
        compiler_params=_cparams(("parallel", "parallel")),
        name="mm_swiglu",
    )(a, w, w)


def _mm_res_kernel(a_ref, w_ref, res_ref, gate_ref, o_ref, *, nb, n_batch):
    acc = _dot(a_ref[...], w_ref[...])
    tm = acc.shape[0]
    for s in range(tm // RB):
        row = _mod_row(pl.program_id(0) * (tm // RB) + s, nb, n_batch)
        gate = gate_ref[pl.ds(row, 1), :]
        rows = slice(s * RB, (s + 1) * RB)
        o_ref[rows, :] = res_ref[rows, :] + gate * acc[rows, :]


def _mm_res(a, w, res, mod, gate_idx, nb, n_batch):
    n, k = a.shape
    m = w.shape[1]
    mr = mod.shape[0]
    tm, tn = _mm_tiles(n, k, m)
    goff = gate_idx * (m // tn)
    return pl.pallas_call(
        functools.partial(_mm_res_kernel, nb=nb, n_batch=n_batch),
        grid=(n // tm, m // tn),
        in_specs=[
            pl.BlockSpec((tm, k), lambda i, j: (i, 0)),
            pl.BlockSpec((k, tn), lambda i, j: (0, j)),
            pl.BlockSpec((tm, tn), lambda i, j: (i, j)),
            pl.BlockSpec((mr, tn), lambda i, j: (0, goff + j)),
        ],
        out_specs=pl.BlockSpec((tm, tn), lambda i, j: (i, j)),
        out_shape=jax.ShapeDtypeStruct((n, m), F32),
        compiler_params=_cparams(("parallel", "parallel")),
        name="mm_residual",
    )(a, w, res, mod)


def _merge_kernel(ya_ref, yb_ref, yc_ref, w_ref, za_ref, zb_ref, zc_ref, o_ref):
    acc = jax.nn.sigmoid(za_ref[...]) * _dot(ya_ref[...], w_ref[0])
    acc += jax.nn.sigmoid(zb_ref[...]) * _dot(yb_ref[...], w_ref[1])
    acc += jax.nn.sigmoid(zc_ref[...]) * _dot(yc_ref[...], w_ref[2])
    o_ref[...] = acc.astype(o_ref.dtype)


def _merge(ya, yb, yc, w_branch, z_mix):
    n, kw = ya.shape
    d = w_branch.shape[2]
    tm = _pick(n, (1024, 512, 256))
    tn = _pick(d, (512, 256, 128))
    nj = d // tn
    yspec = pl.BlockSpec((tm, kw), lambda i, j: (i, 0))
    return pl.pallas_call(
        _merge_kernel,
        grid=(n // tm, nj),
        in_specs=[
            yspec, yspec, yspec,
            pl.BlockSpec((N_BRANCH, kw, tn), lambda i, j: (0, 0, j)),
            pl.BlockSpec((tm, tn), lambda i, j: (i, j)),
            pl.BlockSpec((tm, tn), lambda i, j: (i, nj + j)),
            pl.BlockSpec((tm, tn), lambda i, j: (i, 2 * nj + j)),
        ],
        out_specs=pl.BlockSpec((tm, tn), lambda i, j: (i, j)),
        out_shape=jax.ShapeDtypeStruct((n, d), BF16),
        compiler_params=_cparams(("parallel", "parallel")),
        name="merge",
    )(ya, yb, yc, w_branch, z_mix, z_mix, z_mix)


def _prep_kernel(zr_ref, zp_ref, zn_ref, zs_ref, conv_ref, w0_ref, wup_ref, a0_ref, aup_ref,
                 kk_ref, ka_ref, e_ref,
                 r_ref, v_ref, nkk_ref, kd_ref, bd_ref, lw_ref, up_ref, dn_ref, *, seq, tc):
    rt = zr_ref.shape[0]
    aw = r_ref.shape[1]
    pos = (pl.program_id(0) * rt) % seq
    prev_ok = jnp.logical_and(pos != 0, pos != tc)
    end = pos + rt
    next_ok = jnp.logical_and(end != seq, end != tc)
    u = zr_ref[...]
    up_ref[...] = pltpu.roll(u, 1, 0)
    up_ref[0:1, :] = jnp.where(prev_ok, zp_ref[HALO - 1:HALO, :], 0.0)
    dn_ref[...] = pltpu.roll(u, rt - 1, 0)
    dn_ref[rt - 1:rt, :] = jnp.where(next_ok, zn_ref[0:1, :], 0.0)
    c = up_ref[...] * conv_ref[0:1, :] + u * conv_ref[1:2, :] + dn_ref[...] * conv_ref[2:3, :]
    r = c[:, 0:aw]
    k = c[:, aw:2 * aw]
    v = c[:, 2 * aw:3 * aw]
    r_ref[...] = r
    v_ref[...] = v
    kx = k * kk_ref[...]
    kk = kx * lax.rsqrt(jnp.maximum(_head_sum(kx * kx, e_ref[...]), 1e-24))
    nkk_ref[...] = -kk
    zs = zs_ref[...]
    ka = ka_ref[...]
    for d in range(2):
        zd = zs[:, d * A_RANK:(d + 1) * A_RANK]
        w_raw = w0_ref[d:d + 1, :] + _dot_hl(jnp.tanh(zd), wup_ref[d])
        lw_ref[d] = -math.exp(-0.5) * jax.nn.sigmoid(w_raw)
        za = zs[:, 2 * A_RANK + d * A_RANK:2 * A_RANK + (d + 1) * A_RANK]
        a = jax.nn.sigmoid(a0_ref[d:d + 1, :] + _dot_hl(za, aup_ref[d]))
        kd_ref[d] = k * (1.0 + (a - 1.0) * ka)
        bd_ref[d] = kk * a


def _rwkv_prepare(z_rkv, z_small, conv_w, w0, w_up, a0, a_up, k_k, k_a, e_head, seq, tc):
    n, w3 = z_rkv.shape
    aw = w3 // 3
    rt = 128
    nh = rt // HALO
    last = n // HALO - 1
    full2 = lambda i: (0, 0)
    full3 = lambda i: (0, 0, 0)
    row = jax.ShapeDtypeStruct((n, aw), F32)
    drow = jax.ShapeDtypeStruct((2, n, aw), F32)
    return pl.pallas_call(
        functools.partial(_prep_kernel, seq=seq, tc=tc),
        grid=(n // rt,),
        in_specs=[
            pl.BlockSpec((rt, w3), lambda i: (i, 0)),
            pl.BlockSpec((HALO, w3), lambda i: (jnp.maximum(i * nh - 1, 0), 0)),
            pl.BlockSpec((HALO, w3), lambda i: (jnp.minimum((i + 1) * nh, last), 0)),
            pl.BlockSpec((rt, z_small.shape[1]), lambda i: (i, 0)),
            pl.BlockSpec(conv_w.shape, full2),
            pl.BlockSpec(w0.shape, full2),
            pl.BlockSpec(w_up.shape, full3),
            pl.BlockSpec(a0.shape, full2),
            pl.BlockSpec(a_up.shape, full3),
            pl.BlockSpec((1, aw), full2),
            pl.BlockSpec((1, aw), full2),
            pl.BlockSpec(e_head.shape, full2),
        ],
        out_specs=[
            pl.BlockSpec((rt, aw), lambda i: (i, 0)),
            pl.BlockSpec((rt, aw), lambda i: (i, 0)),
            pl.BlockSpec((rt, aw), lambda i: (i, 0)),
            pl.BlockSpec((2, rt, aw), lambda i: (0, i, 0)),
            pl.BlockSpec((2, rt, aw), lambda i: (0, i, 0)),
            pl.BlockSpec((2, rt, aw), lambda i: (0, i, 0)),
        ],
        out_shape=[row, row, row, drow, drow, drow],
        scratch_shapes=[pltpu.VMEM((rt, w3), F32), pltpu.VMEM((rt, w3), F32)],
        compiler_params=_cparams(("parallel",)),
        name="rwkv_prepare",
    )(z_rkv, z_rkv, z_rkv, z_small, conv_w, w0, w_up, a0, a_up,
      k_k.reshape(1, aw), k_a.reshape(1, aw), e_head)


def _wkv_kernel(rf_ref, vf_ref, af_ref, kf_ref, bf_ref, lwf_ref, rb_ref, vb_ref, ab_ref, kb_ref, bb_ref, lwb_ref,
                yf_ref, yb_ref, s_ref):
    c = rf_ref.shape[0]
    c2 = 2 * c
    n_pair = rf_ref.shape[1] // PAIR

    @pl.when(pl.program_id(1) == 0)
    def _():
        s_ref[...] = jnp.zeros_like(s_ref)

    ti = lax.broadcasted_iota(jnp.int32, (c, c), 0)
    si = lax.broadcasted_iota(jnp.int32, (c, c), 1)
    i2 = lax.broadcasted_iota(jnp.int32, (c2, c2), 0)
    j2 = lax.broadcasted_iota(jnp.int32, (c2, c2), 1)
    col_minus_row = j2 % c - i2 % c
    xr = i2 ^ j2
    eye = (i2 == j2).astype(F32)
    lane = lax.broadcasted_iota(jnp.int32, (1, PAIR), 1)
    m0 = (lane < A_HEAD).astype(F32)
    m1 = 1.0 - m0

    def stack(x):
        return jnp.concatenate([x * m0, x * m1], axis=0)

    zl, zr, v_s, bk_end, dec_end, strict, incl, out = [], [], [], [], [], [], [], []
    for r_ref, v_ref, a_ref, k_ref, b_ref, lw_ref, y_ref, sgn in (
            (rf_ref, vf_ref, af_ref, kf_ref, bf_ref, lwf_ref, yf_ref, 1),
            (rb_ref, vb_ref, ab_ref, kb_ref, bb_ref, lwb_ref, yb_ref, -1)):
        cum = jnp.where((si - ti) * sgn <= 0, 1.0, 0.0).astype(BF16)
        cs_all = _dot_e3(cum, lw_ref[...])
        last = c - 1 if sgn > 0 else 0
        for p in range(n_pair):
            cols = slice(p * PAIR, (p + 1) * PAIR)
            lw = lw_ref[:, cols]
            cs = cs_all[:, cols]
            cs_end = cs[last:last + 1, :]
            g_inv = jnp.exp(-cs)
            g_end = jnp.exp(cs_end - cs)
            kd = k_ref[:, cols]
            bd = b_ref[:, cols]
            a_t = stack(a_ref[:, cols] * jnp.exp(cs - lw))
            r_t = stack(r_ref[:, cols] * jnp.exp(cs))
            zl.append(jnp.concatenate([a_t, r_t], axis=0).astype(BF16))
            zr.append(jnp.concatenate([stack(bd * g_inv), stack(kd * g_inv)], axis=0).astype(BF16))
            v_s.append(stack(v_ref[:, cols]).astype(BF16))
            bk_end.append(jnp.concatenate([stack(bd * g_end), stack(kd * g_end)], axis=0).astype(BF16))
            dec_end.append(jnp.exp(cs_end))
            strict.append(col_minus_row * sgn < 0)
            incl.append(col_minus_row * sgn <= 0)
            out.append((y_ref, cols))

    ch = range(len(zl))
    aa = [_dot_nt(zl[i], zr[i]) for i in ch]
    s0 = [s_ref[i] for i in ch]
    zs = [_dot_nt(zl[i], s0[i].astype(BF16)) for i in ch]
    a_ab = [jnp.where(strict[i], aa[i][0:c2, 0:c2], 0.0) for i in ch]
    av = [_dot(jnp.concatenate([jnp.where(strict[i], aa[i][0:c2, c2:2 * c2], 0.0),
                                jnp.where(incl[i], aa[i][c2:2 * c2, c2:2 * c2], 0.0)], axis=0).astype(BF16),
               v_s[i]) for i in ch]
    a_rb = [jnp.where(incl[i], aa[i][c2:2 * c2, 0:c2], 0.0).astype(BF16) for i in ch]

    n1 = [jnp.where(xr < 8, a_ab[i], 0.0).astype(BF16) for i in ch]
    n2 = [_dot(n1[i], n1[i]).astype(BF16) for i in ch]
    t = [eye + n1[i].astype(F32) for i in ch]
    t = [t[i] + _dot(t[i].astype(BF16), n2[i]) for i in ch]
    n4 = [_dot(n2[i], n2[i]).astype(BF16) for i in ch]
    t = [t[i] + _dot(t[i].astype(BF16), n4[i]) for i in ch]
    m = 8
    while m < c:
        level = jnp.logical_and(xr >= m, xr < 2 * m)
        tb = [t[i].astype(BF16) for i in ch]
        x = [_dot(jnp.where(level, a_ab[i], 0.0).astype(BF16), tb[i]).astype(BF16) for i in ch]
        t = [t[i] + _dot(tb[i], x[i]) for i in ch]
        m *= 2

    u = [_dot(t[i].astype(BF16), (zs[i][0:c2] + av[i][0:c2]).astype(BF16)).astype(BF16) for i in ch]
    yu = [_dot(a_rb[i], u[i]) for i in ch]
    ds = [_dot_tn(jnp.concatenate([u[i], v_s[i]], axis=0), bk_end[i]) for i in ch]
    for i in ch:
        y_s = zs[i][c2:2 * c2] + av[i][c2:2 * c2] + yu[i]
        y_ref, cols = out[i]
        y_ref[:, cols] = y_s[0:c] + y_s[c:c2]
        s_ref[i] = s0[i] * dec_end[i] + ds[i]


def _wkv(r, v, nkk, kd, bd, lw, n_batch, seq, tc):
    n, aw = r.shape
    c = CHUNK
    nch = seq // c
    ncc = tc // c

    def fwd_blk(b, i):
        return b * nch + i

    def bwd_blk(b, i):
        return b * nch + jnp.where(i < ncc, ncc - 1 - i, nch - 1 - (i - ncc))

    def specs(blk, d):
        shared = pl.BlockSpec((c, aw), lambda b, i: (blk(b, i), 0))
        perdir = pl.BlockSpec((None, c, aw), lambda b, i: (d, blk(b, i), 0))
        return [shared, shared, shared, perdir, perdir, perdir], shared

    in_f, out_f = specs(fwd_blk, 0)
    in_b, out_b = specs(bwd_blk, 1)
    y = jax.ShapeDtypeStruct((n, aw), F32)
    return pl.pallas_call(
        _wkv_kernel,
        grid=(n_batch, nch),
        in_specs=in_f + in_b,
        out_specs=[out_f, out_b],
        out_shape=[y, y],
        scratch_shapes=[pltpu.VMEM((2 * (aw // PAIR), PAIR, PAIR), F32)],
        compiler_params=_cparams(("parallel", "arbitrary")),
        name="wkv_scan",
    )(r, v, nkk, kd, bd, lw, r, v, nkk, kd, bd, lw)


def _rwkv_out_kernel(yf_ref, yb_ref, r_ref, v_ref, kd_ref, zs_ref, rk_ref, lw_ref, lb_ref, gup_ref, e_ref, o_ref):
    e = e_ref[...]
    inv = 1.0 / A_HEAD
    y = yf_ref[...] + yb_ref[...]
    mu = _head_sum(y, e) * inv
    yc = y - mu
    var = _head_sum(yc * yc, e) * inv
    yh = yc * lax.rsqrt(var + LNX_EPS) * lw_ref[...] + lb_ref[...]
    bonus = _head_sum(r_ref[...] * (kd_ref[0] + kd_ref[1]) * rk_ref[...], e) * v_ref[...]
    zg = zs_ref[...][:, 2 * 2 * A_RANK:2 * 2 * A_RANK + A_GATE_RANK]
    g = _dot_hl(jax.nn.sigmoid(zg), gup_ref[...])
    o_ref[...] = ((yh + bonus) * g).astype(o_ref.dtype)


def _rwkv_output(yf, yb, r, v, kd, z_small, r_k, lnx_w, lnx_b, gate_up, e_head):
    n, aw = r.shape
    rt = RB
    rowspec = pl.BlockSpec((rt, aw), lambda i: (i, 0))
    dspec = pl.BlockSpec((2, rt, aw), lambda i: (0, i, 0))
    vec = pl.BlockSpec((1, aw), lambda i: (0, 0))
    return pl.pallas_call(
        _rwkv_out_kernel,
        grid=(n // rt,),
        in_specs=[
            rowspec, rowspec, rowspec, rowspec, dspec,
            pl.BlockSpec((rt, z_small.shape[1]), lambda i: (i, 0)),
            vec, vec, vec,
            pl.BlockSpec(gate_up.shape, lambda i: (0, 0)),
            pl.BlockSpec(e_head.shape, lambda i: (0, 0)),
        ],
        out_specs=rowspec,
        out_shape=jax.ShapeDtypeStruct((n, aw), BF16),
        compiler_params=_cparams(("parallel",)),
        name="rwkv_output",
    )(yf, yb, r, v, kd, z_small, r_k.reshape(1, aw), lnx_w.reshape(1, aw), lnx_b.reshape(1, aw),
      gate_up, e_head)


def _pool_kernel(u_ref, w_ref, ps_ref, o_ref, pad_ref, *, tc):
    seq = u_ref.shape[0]
    t_lat = seq - tc
    half_max = max(B_WINDOWS) // 2
    lat0 = half_max + tc + 2 * half_max
    gw = u_ref.shape[1]
    zeros = jnp.zeros((2 * half_max, gw), F32)
    pad_ref[0:half_max, :] = zeros[0:half_max]
    pad_ref[half_max:half_max + tc, :] = u_ref[0:tc, :]
    pad_ref[half_max + tc:lat0, :] = zeros
    pad_ref[lat0:lat0 + t_lat, :] = u_ref[tc:seq, :]
    pad_ref[lat0 + t_lat:lat0 + t_lat + half_max, :] = zeros[0:half_max]

    def segment(start, n, win):
        half = win // 2
        acc = pad_ref[start - half:start - half + n, :]
        for j in range(1 - half, half):
            acc = acc + pad_ref[start + j:start + j + n, :]
        t = lax.broadcasted_iota(jnp.int32, (n, gw), 0)
        cnt = jnp.minimum(t + half, n) - jnp.maximum(t - half, 0)
        return acc / cnt.astype(F32) - pad_ref[start:start + n, :]

    for gi, win in enumerate(B_WINDOWS):
        @pl.when(pl.program_id(1) == gi)
        def _(win=win):
            w = w_ref[0]
            ps = ps_ref[...]
            o_ref[0:tc, :] = (_dot(segment(half_max, tc, win).astype(BF16), w) * ps).astype(o_ref.dtype)
            o_ref[tc:seq, :] = (_dot(segment(lat0, t_lat, win).astype(BF16), w) * ps).astype(o_ref.dtype)


def _pool(z_pool, pool_w, pool_scale, n_batch, seq, tc):
    n, bw = z_pool.shape
    ng = len(B_WINDOWS)
    gw = bw // ng
    pad_rows = seq + 4 * (max(B_WINDOWS) // 2)
    return pl.pallas_call(
        functools.partial(_pool_kernel, tc=tc),
        grid=(n_batch, ng),
        in_specs=[
            pl.BlockSpec((seq, gw), lambda b, g: (b, g)),
            pl.BlockSpec((1, gw, gw), lambda b, g: (g, 0, 0)),
            pl.BlockSpec((1, gw), lambda b, g: (0, g)),
        ],
        out_specs=pl.BlockSpec((seq, gw), lambda b, g: (b, g)),
        out_shape=jax.ShapeDtypeStruct((n, bw), BF16),
        scratch_shapes=[pltpu.VMEM((pad_rows, gw), F32)],
        compiler_params=_cparams(("parallel", "parallel")),
        name="pool_mixer",
    )(z_pool, pool_w, pool_scale.reshape(1, bw))


def _attn_kernel(q_ref, k_ref, v_ref, lam_ref, g_ref, o_ref, *, tc, lam_init):
    lq = lam_ref[...]
    lam = (jnp.exp(jnp.sum(lq[0:1] * lq[1:2], axis=-1, keepdims=True))
           - jnp.exp(jnp.sum(lq[2:3] * lq[3:4], axis=-1, keepdims=True)) + lam_init)
    lane = lax.broadcasted_iota(jnp.int32, (1, C_VHEAD), 1)
    q = q_ref[...]
    zero = jnp.zeros_like(q)
    q0 = jnp.where(lane < C_HEAD, q, zero)
    q1 = jnp.where(lane < C_HEAD, zero, q)
    g = g_ref[...] * (1.0 - lam_init)

    def attend(nk):
        k = k_ref[0:nk, :]

        def probs(qh):
            s = _dot_nt(qh, k)
            e = jnp.exp(s - jnp.max(s, axis=-1, keepdims=True))
            return e, 1.0 / jnp.sum(e, axis=-1, keepdims=True)

        e0, i0 = probs(q0)
        e1, i1 = probs(q1)
        amap = (e0 * i0 - e1 * (lam * i1)).astype(BF16)
        o = _dot(amap, v_ref[0:nk, :])
        o = o * lax.rsqrt(jnp.mean(o * o, axis=-1, keepdims=True) + 1e-5) * g
        o_ref[...] = o.astype(o_ref.dtype)

    @pl.when(pl.program_id(2) == 0)
    def _():
        attend(tc)

    @pl.when(pl.program_id(2) != 0)
    def _():
        attend(k_ref.shape[0])


def _attention(q, k, v, lam_qk, subln_g, lam_init, n_batch, seq, tc):
    n, w = q.shape
    nh = w // C_VHEAD
    nb = seq // RB
    kv = pl.BlockSpec((seq, C_VHEAD), lambda b, h, i: (b, h))
    qo = pl.BlockSpec((RB, C_VHEAD), lambda b, h, i: (b * nb + i, h))
    return pl.pallas_call(
        functools.partial(_attn_kernel, tc=tc, lam_init=lam_init),
        grid=(n_batch, nh, nb),
        in_specs=[
            qo, kv, kv,
            pl.BlockSpec(lam_qk.shape, lambda b, h, i: (0, 0)),
            pl.BlockSpec((1, C_VHEAD), lambda b, h, i: (0, 0)),
        ],
        out_specs=qo,
        out_shape=jax.ShapeDtypeStruct((n, w), BF16),
        compiler_params=_cparams(("parallel", "parallel", "parallel")),
        name="diff_attention",
    )(q, k, v, lam_qk, subln_g.reshape(1, C_VHEAD))


def _rope_tables(n_batch, t_lat, tc):
    n_freq = C_HEAD // 4
    inv = ROPE_BASE ** (-jnp.arange(n_freq, dtype=F32) / n_freq)
    rows = t_lat // GRID_W
    t_row = jnp.repeat(jnp.arange(rows, dtype=F32), GRID_W)
    t_col = jnp.tile(jnp.arange(GRID_W, dtype=F32), rows)
    ang_r = t_row[:, None] * inv
    ang_c = t_col[:, None] * inv
    cos = jnp.concatenate([jnp.cos(ang_r)] * 2 + [jnp.cos(ang_c)] * 2, axis=1)
    sin = jnp.concatenate([-jnp.sin(ang_r), jnp.sin(ang_r), -jnp.sin(ang_c), jnp.sin(ang_c)], axis=1)
    cos = jnp.concatenate([jnp.ones((tc, C_HEAD), F32), cos], axis=0)
    sin = jnp.concatenate([jnp.zeros((tc, C_HEAD), F32), sin], axis=0)
    cos = jnp.tile(jnp.tile(cos, (1, LANES // C_HEAD)), (n_batch, 1))
    sin = jnp.tile(jnp.tile(sin, (1, LANES // C_HEAD)), (n_batch, 1))
    return cos, sin


def kernel(x, c, ctx, c_ctx, w_ada, b_ada, norm_g, w_in, rkv_conv, decay_w0, decay_up, iclr_a0, iclr_up, gate_up, k_k, k_a, r_k, lnx_w, lnx_b, pool_w, pool_scale, lam_qk, subln_g, w_branch, w_out, w_ffn_in, w_ffn_out, final_g):
    n_batch, t_lat, d = x.shape
    tc = ctx.shape[1]
    depth = w_ada.shape[0]
    aw = k_k.shape[1]
    assert tc == RB and t_lat % RB == 0 and t_lat % GRID_W == 0 and aw % PAIR == 0
    seq = tc + t_lat
    nb = seq // RB
    n = n_batch * seq

    xa = jnp.concatenate([ctx, x], axis=1).reshape(n, d)
    mr = -(-(n_batch + 1) // SUBLANES) * SUBLANES
    c_all = jnp.zeros((mr, d), F32).at[:n_batch].set(c).at[n_batch].set(c_ctx)
    mods = _mod_tables(c_all, w_ada, b_ada)
    cos_t, sin_t = _rope_tables(n_batch, t_lat, tc)
    hid = lax.broadcasted_iota(jnp.int32, (MXU_COLS, MXU_COLS), 0) // A_HEAD
    e_head = (hid == hid.T).astype(BF16)

    o_small = 3 * aw
    o_pool = o_small + 2 * 2 * A_RANK + A_GATE_RANK
    o_q = o_pool + aw
    o_k = o_q + aw
    o_v = o_k + aw
    o_mix = o_v + aw

    for l in range(depth):
        lam_init = 0.8 - 0.6 * math.exp(-0.3 * l)
        mod = mods[l]
        wi = w_in[l].astype(BF16)

        h = _norm_mod(xa, norm_g[l, 0], mod, 0, 1, nb, n_batch)
        z_rkv = _mm(h, wi[:, 0:o_small], F32)
        z_small = _mm(h, wi[:, o_small:o_pool], F32)
        z_pool = _mm(h, wi[:, o_pool:o_q], F32)
        q = _mm_rope(h, wi[:, o_q:o_k], cos_t, sin_t, C_HEAD ** -0.5, BF16)
        k = _mm_rope(h, wi[:, o_k:o_v], cos_t, sin_t, 1.0, BF16)
        v = _mm(h, wi[:, o_v:o_mix], BF16)
        z_mix = _mm(h, wi[:, o_mix:], F32)

        r, va, nkk, kd, bd, lw = _rwkv_prepare(
            z_rkv, z_small, rkv_conv[l], decay_w0[l], decay_up[l], iclr_a0[l], iclr_up[l],
            k_k[l], k_a[l], e_head, seq, tc)
        yf, yb_dir = _wkv(r, va, nkk, kd, bd, lw, n_batch, seq, tc)
        ya = _rwkv_output(yf, yb_dir, r, va, kd, z_small, r_k[l], lnx_w[l], lnx_b[l], gate_up[l], e_head)
        yb = _pool(z_pool, pool_w[l].astype(BF16), pool_scale[l], n_batch, seq, tc)
        yc = _attention(q, k, v, lam_qk[l], subln_g[l], lam_init, n_batch, seq, tc)

        acc = _merge(ya, yb, yc, w_branch[l].astype(BF16), z_mix)
        xa = _mm_res(acc, w_out[l].astype(BF16), xa, mod, 2, nb, n_batch)

        h2 = _norm_mod(xa, norm_g[l, 1], mod, 3, 4, nb, n_batch)
        ff = _mm_swiglu(h2, w_ffn_in[l].astype(BF16), BF16)
        xa = _mm_res(ff, w_ffn_out[l].astype(BF16), xa, mod, 5, nb, n_batch)

    out = _final_norm(xa, final_g, nb, n_batch)
    return out.reshape(n_batch, t_lat, d)
```

```python
import functools
import math

import jax
import jax.numpy as jnp
from jax import lax
from jax.experimental import pallas as pl
from jax.experimental.pallas import tpu as pltpu

F32 = jnp.float32
BF16 = jnp.bfloat16

LANES = 128
SUBLANES = 8
MXU_COLS = 256
VMEM_LIMIT = 56 * 1024 * 1024

A_HEAD = 64
A_RANK = 64
A_GATE_RANK = 128
LNX_EPS = 64e-5
B_WINDOWS = (2, 4, 8, 16)
C_HEAD = 64
C_VHEAD = 2 * C_HEAD
ROPE_BASE = 10000.0
GRID_W = 64
N_MOD = 6
N_BRANCH = 3
RB = 256
CHUNK = 64
PAIR = 2 * A_HEAD
HALO = SUBLANES


def _cparams(sem):
    return pltpu.CompilerParams(dimension_semantics=sem, vmem_limit_bytes=VMEM_LIMIT)


def _pick(n, cands):
    for c in cands:
        if n % c == 0:
            return c
    raise ValueError(f"no tile for {n} in {cands}")


def _mod_row(blk, nb, n_batch):
    return jnp.where(blk % nb == 0, n_batch, blk // nb)


def _split3(x):
    hi = x.astype(BF16)
    r1 = x - hi.astype(F32)
    mid = r1.astype(BF16)
    lo = (r1 - mid.astype(F32)).astype(BF16)
    return hi, mid, lo


def _dot(a, b):
    return jnp.dot(a, b, preferred_element_type=F32)


def _dot_nt(a, b):
    return lax.dot_general(a, b, (((1,), (1,)), ((), ())), preferred_element_type=F32)


def _dot_tn(a, b):
    return lax.dot_general(a, b, (((0,), (0,)), ((), ())), preferred_element_type=F32)


def _head_sum(x, e):
    hi = x.astype(BF16)
    lo = (x - hi.astype(F32)).astype(BF16)
    w = e.shape[0]
    return jnp.concatenate(
        [_dot(hi[:, s:s + w], e) + _dot(lo[:, s:s + w], e) for s in range(0, x.shape[1], w)], axis=1)


def _dot_e3(e, x):
    hi, mid, lo = _split3(x)
    return _dot(e, hi) + _dot(e, mid) + _dot(e, lo)


def _dot_hl(x, w):
    xh = x.astype(BF16)
    xl = (x - xh.astype(F32)).astype(BF16)
    wh = w.astype(BF16)
    wl = (w - wh.astype(F32)).astype(BF16)
    return _dot(xh, wh) + _dot(xh, wl) + _dot(xl, wh)


def _mod_kernel(c_ref, w_ref, b_ref, o_ref):
    c = c_ref[...]
    s = (c * jax.nn.sigmoid(c)).astype(BF16)
    o_ref[0] = _dot(s, w_ref[0].astype(BF16)) + b_ref[0]


def _mod_tables(c_all, w_ada, b_ada):
    nl, d, n6 = w_ada.shape
    mr = c_all.shape[0]
    tn = _pick(n6, (1024, 512, 256, 128))
    return pl.pallas_call(
        _mod_kernel,
        grid=(nl, n6 // tn),
        in_specs=[
            pl.BlockSpec((mr, d), lambda l, j: (0, 0)),
            pl.BlockSpec((1, d, tn), lambda l, j: (l, 0, j)),
            pl.BlockSpec((1, 1, tn), lambda l, j: (l, 0, j)),
        ],
        out_specs=pl.BlockSpec((1, mr, tn), lambda l, j: (l, 0, j)),
        out_shape=jax.ShapeDtypeStruct((nl, mr, n6), F32),
        compiler_params=_cparams(("parallel", "parallel")),
        name="adaln_table",
    )(c_all, w_ada, b_ada.reshape(nl, 1, n6))


def _norm_kernel(x_ref, g_ref, shift_ref, scale_ref, o_ref, *, nb, n_batch):
    row = _mod_row(pl.program_id(0), nb, n_batch)
    x = x_ref[...]
    y = x * lax.rsqrt(jnp.mean(x * x, axis=-1, keepdims=True) + 1e-6) * g_ref[...]
    y = y * (1.0 + scale_ref[pl.ds(row, 1), :]) + shift_ref[pl.ds(row, 1), :]
    o_ref[...] = y.astype(o_ref.dtype)


def _norm_mod(x, g, mod, shift_idx, scale_idx, nb, n_batch):
    n, d = x.shape
    mr = mod.shape[0]
    return pl.pallas_call(
        functools.partial(_norm_kernel, nb=nb, n_batch=n_batch),
        grid=(n // RB,),
        in_specs=[
            pl.BlockSpec((RB, d), lambda i: (i, 0)),
            pl.BlockSpec((1, d), lambda i: (0, 0)),
            pl.BlockSpec((mr, d), lambda i: (0, shift_idx)),
            pl.BlockSpec((mr, d), lambda i: (0, scale_idx)),
        ],
        out_specs=pl.BlockSpec((RB, d), lambda i: (i, 0)),
        out_shape=jax.ShapeDtypeStruct((n, d), BF16),
        compiler_params=_cparams(("parallel",)),
        name="norm_mod",
    )(x, g.reshape(1, d), mod, mod)


def _final_norm_kernel(x_ref, g_ref, o_ref):
    x = x_ref[...]
    o_ref[...] = x * lax.rsqrt(jnp.mean(x * x, axis=-1, keepdims=True) + 1e-6) * g_ref[...]


def _final_norm(x, g, nb, n_batch):
    n, d = x.shape
    nlb = nb - 1
    return pl.pallas_call(
        _final_norm_kernel,
        grid=(n_batch, nlb),
        in_specs=[
            pl.BlockSpec((RB, d), lambda b, i: (b * nb + i + 1, 0)),
            pl.BlockSpec((1, d), lambda b, i: (0, 0)),
        ],
        out_specs=pl.BlockSpec((RB, d), lambda b, i: (b * nlb + i, 0)),
        out_shape=jax.ShapeDtypeStruct((n_batch * nlb * RB, d), F32),
        compiler_params=_cparams(("parallel", "parallel")),
        name="final_norm",
    )(x, g.reshape(1, d))


def _mm_plain_kernel(a_ref, w_ref, o_ref):
    o_ref[...] = _dot(a_ref[...], w_ref[...]).astype(o_ref.dtype)


def _mm_tiles(n, k, m):
    tn = _pick(m, (1024, 512, 384, 256, 128)) if k <= 2048 else _pick(m, (512, 256, 128))
    tm = _pick(n, (2048, 1024, 512, 256)) if tn <= 512 and k <= 2048 else _pick(n, (1024, 512, 256))
    return tm, tn


def _mm(a, w, out_dtype):
    n, k = a.shape
    m = w.shape[1]
    tm, tn = _mm_tiles(n, k, m)
    return pl.pallas_call(
        _mm_plain_kernel,
        grid=(n // tm, m // tn),
        in_specs=[
            pl.BlockSpec((tm, k), lambda i, j: (i, 0)),
            pl.BlockSpec((k, tn), lambda i, j: (0, j)),
        ],
        out_specs=pl.BlockSpec((tm, tn), lambda i, j: (i, j)),
        out_shape=jax.ShapeDtypeStruct((n, m), out_dtype),
        compiler_params=_cparams(("parallel", "parallel")),
        name="mm_plain",
    )(a, w)


def _mm_rope_kernel(a_ref, w_ref, cos_ref, sin_ref, o_ref, *, scale):
    acc = _dot(a_ref[...], w_ref[...])
    lane = lax.broadcasted_iota(jnp.int32, (1, LANES), 1)
    first_half = (lane % (C_HEAD // 2)) < (C_HEAD // 4)
    cos = cos_ref[...]
    sin = sin_ref[...]
    for s in range(acc.shape[1] // LANES):
        u = acc[:, s * LANES:(s + 1) * LANES]
        partner = jnp.where(first_half,
                            pltpu.roll(u, LANES - C_HEAD // 4, 1),
                            pltpu.roll(u, C_HEAD // 4, 1))
        o_ref[:, s * LANES:(s + 1) * LANES] = ((u * cos + partner * sin) * scale).astype(o_ref.dtype)


def _mm_rope(a, w, cos_t, sin_t, scale, out_dtype):
    n, k = a.shape
    m = w.shape[1]
    tm, tn = _mm_tiles(n, k, m)
    return pl.pallas_call(
        functools.partial(_mm_rope_kernel, scale=scale),
        grid=(n // tm, m // tn),
        in_specs=[
            pl.BlockSpec((tm, k), lambda i, j: (i, 0)),
            pl.BlockSpec((k, tn), lambda i, j: (0, j)),
            pl.BlockSpec((tm, LANES), lambda i, j: (i, 0)),
            pl.BlockSpec((tm, LANES), lambda i, j: (i, 0)),
        ],
        out_specs=pl.BlockSpec((tm, tn), lambda i, j: (i, j)),
        out_shape=jax.ShapeDtypeStruct((n, m), out_dtype),
        compiler_params=_cparams(("parallel", "parallel")),
        name="mm_rope",
    )(a, w, cos_t, sin_t)


def _mm_swiglu_kernel(a_ref, wg_ref, wu_ref, o_ref):
    a = a_ref[...]
    g = _dot(a, wg_ref[...])
    u = _dot(a, wu_ref[...])
    o_ref[...] = (g * jax.nn.sigmoid(g) * u).astype(o_ref.dtype)


def _mm_swiglu(a, w, out_dtype):
    n, k = a.shape
    f = w.shape[1] // 2
    tm, tn = _mm_tiles(n, k, f)
    nj = f // tn
    return pl.pallas_call(
        _mm_swiglu_kernel,
        grid=(n // tm, nj),
        in_specs=[
            pl.BlockSpec((tm, k), lambda i, j: (i, 0)),
            pl.BlockSpec((k, tn), lambda i, j: (0, j)),
            pl.BlockSpec((k, tn), lambda i, j: (0, j + nj)),
        ],
        out_specs=pl.BlockSpec((tm, tn), lambda i, j: (i, j)),
        out_shape=jax.ShapeDtypeStruct((n, f), out_dtype),
        compiler_params=_cparams(("parallel", "parallel")),
        name="mm_swiglu",
    )(a, w, w)


def _mm_res_kernel(a_ref, w_ref, res_ref, gate_ref, o_ref, *, nb, n_batch):
    acc = _dot(a_ref[...], w_ref[...])
    tm = acc.shape[0]
    for s in range(tm // RB):
        row = _mod_row(pl.program_id(0) * (tm // RB) + s, nb, n_batch)
        gate = gate_ref[pl.ds(row, 1), :]
        rows = slice(s * RB, (s + 1) * RB)
        o_ref[rows, :] = res_ref[rows, :] + gate * acc[rows, :]


def _mm_res(a, w, res, mod, gate_idx, nb, n_batch):
    n, k = a.shape
    m = w.shape[1]
    mr = mod.shape[0]
    tm, tn = _mm_tiles(n, k, m)
    goff = gate_idx * (m // tn)
    return pl.pallas_call(
        functools.partial(_mm_res_kernel, nb=nb, n_batch=n_batch),
        grid=(n // tm, m // tn),
        in_specs=[
            pl.BlockSpec((tm, k), lambda i, j: (i, 0)),
            pl.BlockSpec((k, tn), lambda i, j: (0, j)),
            pl.BlockSpec((tm, tn), lambda i, j: (i, j)),
            pl.BlockSpec((mr, tn), lambda i, j: (0, goff + j)),
        ],
        out_specs=pl.BlockSpec((tm, tn), lambda i, j: (i, j)),
        out_shape=jax.ShapeDtypeStruct((n, m), F32),
        compiler_params=_cparams(("parallel", "parallel")),
        name="mm_residual",
    )(a, w, res, mod)


def _mm_res_norm_kernel(a_ref, w_ref, res_ref, gate_ref, g_ref, shift_ref, scale_ref, x_ref, h_ref, *, nb, n_batch):
    acc = _dot(a_ref[...], w_ref[...])
    tm = acc.shape[0]
    for s in range(tm // RB):
        row = _mod_row(pl.program_id(0) * (tm // RB) + s, nb, n_batch)
        rows = slice(s * RB, (s + 1) * RB)
        x = res_ref[rows, :] + gate_ref[pl.ds(row, 1), :] * acc[rows, :]
        x_ref[rows, :] = x
        y = x * lax.rsqrt(jnp.mean(x * x, axis=-1, keepdims=True) + 1e-6) * g_ref[...]
        h_ref[rows, :] = (y * (1.0 + scale_ref[pl.ds(row, 1), :]) + shift_ref[pl.ds(row, 1), :]).astype(h_ref.dtype)


def _mm_res_norm(a, w, res, mod, gate_idx, g_next, mod_next, shift_idx, scale_idx, nb, n_batch):
    n, k = a.shape
    d = w.shape[1]
    mr = mod.shape[0]
    tm = _pick(n, (512, 256)) if k <= 2048 else RB
    row = lambda i: (i, 0)
    return pl.pallas_call(
        functools.partial(_mm_res_norm_kernel, nb=nb, n_batch=n_batch),
        grid=(n // tm,),
        in_specs=[
            pl.BlockSpec((tm, k), row),
            pl.BlockSpec((k, d), lambda i: (0, 0), pipeline_mode=pl.Buffered(1)),
            pl.BlockSpec((tm, d), row),
            pl.BlockSpec((mr, d), lambda i: (0, gate_idx)),
            pl.BlockSpec((1, d), lambda i: (0, 0)),
            pl.BlockSpec((mr, d), lambda i: (0, shift_idx)),
            pl.BlockSpec((mr, d), lambda i: (0, scale_idx)),
        ],
        out_specs=[pl.BlockSpec((tm, d), row), pl.BlockSpec((tm, d), row)],
        out_shape=[jax.ShapeDtypeStruct((n, d), F32), jax.ShapeDtypeStruct((n, d), BF16)],
        compiler_params=_cparams(("parallel",)),
        name="mm_residual_norm",
    )(a, w, res, mod, g_next.reshape(1, d), mod_next, mod_next)


def _merge_kernel(ya_ref, yb_ref, yc_ref, w_ref, za_ref, zb_ref, zc_ref, o_ref):
    acc = jax.nn.sigmoid(za_ref[...]) * _dot(ya_ref[...], w_ref[0])
    acc += jax.nn.sigmoid(zb_ref[...]) * _dot(yb_ref[...], w_ref[1])
    acc += jax.nn.sigmoid(zc_ref[...]) * _dot(yc_ref[...], w_ref[2])
    o_ref[...] = acc.astype(o_ref.dtype)


def _merge(ya, yb, yc, w_branch, z_mix):
    n, kw = ya.shape
    d = w_branch.shape[2]
    tm = _pick(n, (1024, 512, 256))
    tn = _pick(d, (512, 256, 128))
    nj = d // tn
    yspec = pl.BlockSpec((tm, kw), lambda i, j: (i, 0))
    return pl.pallas_call(
        _merge_kernel,
        grid=(n // tm, nj),
        in_specs=[
            yspec, yspec, yspec,
            pl.BlockSpec((N_BRANCH, kw, tn), lambda i, j: (0, 0, j)),
            pl.BlockSpec((tm, tn), lambda i, j: (i, j)),
            pl.BlockSpec((tm, tn), lambda i, j: (i, nj + j)),
            pl.BlockSpec((tm, tn), lambda i, j: (i, 2 * nj + j)),
        ],
        out_specs=pl.BlockSpec((tm, tn), lambda i, j: (i, j)),
        out_shape=jax.ShapeDtypeStruct((n, d), BF16),
        compiler_params=_cparams(("parallel", "parallel")),
        name="merge",
    )(ya, yb, yc, w_branch, z_mix, z_mix, z_mix)


def _prep_kernel(zr_ref, zp_ref, zn_ref, zs_ref, conv_ref, w0_ref, wup_ref, a0_ref, aup_ref,
                 kk_ref, ka_ref, e_ref,
                 r_ref, v_ref, nkk_ref, kd_ref, bd_ref, lw_ref, up_ref, dn_ref, *, seq, tc):
    rt = zr_ref.shape[0]
    aw = r_ref.shape[1]
    pos = (pl.program_id(0) * rt) % seq
    prev_ok = jnp.logical_and(pos != 0, pos != tc)
    end = pos + rt
    next_ok = jnp.logical_and(end != seq, end != tc)
    u = zr_ref[...]
    up_ref[...] = pltpu.roll(u, 1, 0)
    up_ref[0:1, :] = jnp.where(prev_ok, zp_ref[HALO - 1:HALO, :], 0.0)
    dn_ref[...] = pltpu.roll(u, rt - 1, 0)
    dn_ref[rt - 1:rt, :] = jnp.where(next_ok, zn_ref[0:1, :], 0.0)
    c = up_ref[...] * conv_ref[0:1, :] + u * conv_ref[1:2, :] + dn_ref[...] * conv_ref[2:3, :]
    r = c[:, 0:aw]
    k = c[:, aw:2 * aw]
    v = c[:, 2 * aw:3 * aw]
    r_ref[...] = r
    v_ref[...] = v
    kx = k * kk_ref[...]
    kk = kx * lax.rsqrt(jnp.maximum(_head_sum(kx * kx, e_ref[...]), 1e-24))
    nkk_ref[...] = -kk
    zs = zs_ref[...]
    ka = ka_ref[...]
    for d in range(2):
        zd = zs[:, d * A_RANK:(d + 1) * A_RANK]
        w_raw = w0_ref[d:d + 1, :] + _dot_hl(jnp.tanh(zd), wup_ref[d])
        lw_ref[d] = -math.exp(-0.5) * jax.nn.sigmoid(w_raw)
        za = zs[:, 2 * A_RANK + d * A_RANK:2 * A_RANK + (d + 1) * A_RANK]
        a = jax.nn.sigmoid(a0_ref[d:d + 1, :] + _dot_hl(za, aup_ref[d]))
        kd_ref[d] = k * (1.0 + (a - 1.0) * ka)
        bd_ref[d] = kk * a


def _rwkv_prepare(z_rkv, z_small, conv_w, w0, w_up, a0, a_up, k_k, k_a, e_head, seq, tc):
    n, w3 = z_rkv.shape
    aw = w3 // 3
    rt = 128
    nh = rt // HALO
    last = n // HALO - 1
    full2 = lambda i: (0, 0)
    full3 = lambda i: (0, 0, 0)
    row = jax.ShapeDtypeStruct((n, aw), F32)
    drow = jax.ShapeDtypeStruct((2, n, aw), F32)
    return pl.pallas_call(
        functools.partial(_prep_kernel, seq=seq, tc=tc),
        grid=(n // rt,),
        in_specs=[
            pl.BlockSpec((rt, w3), lambda i: (i, 0)),
            pl.BlockSpec((HALO, w3), lambda i: (jnp.maximum(i * nh - 1, 0), 0)),
            pl.BlockSpec((HALO, w3), lambda i: (jnp.minimum((i + 1) * nh, last), 0)),
            pl.BlockSpec((rt, z_small.shape[1]), lambda i: (i, 0)),
            pl.BlockSpec(conv_w.shape, full2),
            pl.BlockSpec(w0.shape, full2),
            pl.BlockSpec(w_up.shape, full3),
            pl.BlockSpec(a0.shape, full2),
            pl.BlockSpec(a_up.shape, full3),
            pl.BlockSpec((1, aw), full2),
            pl.BlockSpec((1, aw), full2),
            pl.BlockSpec(e_head.shape, full2),
        ],
        out_specs=[
            pl.BlockSpec((rt, aw), lambda i: (i, 0)),
            pl.BlockSpec((rt, aw), lambda i: (i, 0)),
            pl.BlockSpec((rt, aw), lambda i: (i, 0)),
            pl.BlockSpec((2, rt, aw), lambda i: (0, i, 0)),
            pl.BlockSpec((2, rt, aw), lambda i: (0, i, 0)),
            pl.BlockSpec((2, rt, aw), lambda i: (0, i, 0)),
        ],
        out_shape=[row, row, row, drow, drow, drow],
        scratch_shapes=[pltpu.VMEM((rt, w3), F32), pltpu.VMEM((rt, w3), F32)],
        compiler_params=_cparams(("parallel",)),
        name="rwkv_prepare",
    )(z_rkv, z_rkv, z_rkv, z_small, conv_w, w0, w_up, a0, a_up,
      k_k.reshape(1, aw), k_a.reshape(1, aw), e_head)


def _wkv_kernel(rf_ref, vf_ref, af_ref, kf_ref, bf_ref, lwf_ref, rb_ref, vb_ref, ab_ref, kb_ref, bb_ref, lwb_ref,
                yf_ref, yb_ref, s_ref):
    c = rf_ref.shape[0]
    c2 = 2 * c
    n_pair = rf_ref.shape[1] // PAIR

    @pl.when(pl.program_id(1) == 0)
    def _():
        s_ref[...] = jnp.zeros_like(s_ref)

    ti = lax.broadcasted_iota(jnp.int32, (c, c), 0)
    si = lax.broadcasted_iota(jnp.int32, (c, c), 1)
    i2 = lax.broadcasted_iota(jnp.int32, (c2, c2), 0)
    j2 = lax.broadcasted_iota(jnp.int32, (c2, c2), 1)
    col_minus_row = j2 % c - i2 % c
    xr = i2 ^ j2
    eye = (i2 == j2).astype(F32)
    lane = lax.broadcasted_iota(jnp.int32, (1, PAIR), 1)
    m0 = (lane < A_HEAD).astype(F32)
    m1 = 1.0 - m0

    def stack(x):
        return jnp.concatenate([x * m0, x * m1], axis=0)

    zl, zr, v_s, bk_end, dec_end, strict, incl, out = [], [], [], [], [], [], [], []
    for r_ref, v_ref, a_ref, k_ref, b_ref, lw_ref, y_ref, sgn in (
            (rf_ref, vf_ref, af_ref, kf_ref, bf_ref, lwf_ref, yf_ref, 1),
            (rb_ref, vb_ref, ab_ref, kb_ref, bb_ref, lwb_ref, yb_ref, -1)):
        cum = jnp.where((si - ti) * sgn <= 0, 1.0, 0.0).astype(BF16)
        cs_all = _dot_e3(cum, lw_ref[...])
        last = c - 1 if sgn > 0 else 0
        for p in range(n_pair):
            cols = slice(p * PAIR, (p + 1) * PAIR)
            lw = lw_ref[:, cols]
            cs = cs_all[:, cols]
            cs_end = cs[last:last + 1, :]
            g_inv = jnp.exp(-cs)
            g_end = jnp.exp(cs_end - cs)
            kd = k_ref[:, cols]
            bd = b_ref[:, cols]
            a_t = stack(a_ref[:, cols] * jnp.exp(cs - lw))
            r_t = stack(r_ref[:, cols] * jnp.exp(cs))
            zl.append(jnp.concatenate([a_t, r_t], axis=0).astype(BF16))
            zr.append(jnp.concatenate([stack(bd * g_inv), stack(kd * g_inv)], axis=0).astype(BF16))
            v_s.append(stack(v_ref[:, cols]).astype(BF16))
            bk_end.append(jnp.concatenate([stack(bd * g_end), stack(kd * g_end)], axis=0).astype(BF16))
            dec_end.append(jnp.exp(cs_end))
            strict.append(col_minus_row * sgn < 0)
            incl.append(col_minus_row * sgn <= 0)
            out.append((y_ref, cols))

    ch = range(len(zl))
    aa = [_dot_nt(zl[i], zr[i]) for i in ch]
    s0 = [s_ref[i] for i in ch]
    zs = [_dot_nt(zl[i], s0[i].astype(BF16)) for i in ch]
    a_ab = [jnp.where(strict[i], aa[i][0:c2, 0:c2], 0.0) for i in ch]
    av = [_dot(jnp.concatenate([jnp.where(strict[i], aa[i][0:c2, c2:2 * c2], 0.0),
                                jnp.where(incl[i], aa[i][c2:2 * c2, c2:2 * c2], 0.0)], axis=0).astype(BF16),
               v_s[i]) for i in ch]
    a_rb = [jnp.where(incl[i], aa[i][c2:2 * c2, 0:c2], 0.0).astype(BF16) for i in ch]

    n1 = [jnp.where(xr < 8, a_ab[i], 0.0).astype(BF16) for i in ch]
    n2 = [_dot(n1[i], n1[i]).astype(BF16) for i in ch]
    t = [eye + n1[i].astype(F32) for i in ch]
    t = [t[i] + _dot(t[i].astype(BF16), n2[i]) for i in ch]
    n4 = [_dot(n2[i], n2[i]).astype(BF16) for i in ch]
    t = [t[i] + _dot(t[i].astype(BF16), n4[i]) for i in ch]
    m = 8
    while m < c:
        level = jnp.logical_and(xr >= m, xr < 2 * m)
        tb = [t[i].astype(BF16) for i in ch]
        x = [_dot(jnp.where(level, a_ab[i], 0.0).astype(BF16), tb[i]).astype(BF16) for i in ch]
        t = [t[i] + _dot(tb[i], x[i]) for i in ch]
        m *= 2

    u = [_dot(t[i].astype(BF16), (zs[i][0:c2] + av[i][0:c2]).astype(BF16)).astype(BF16) for i in ch]
    yu = [_dot(a_rb[i], u[i]) for i in ch]
    ds = [_dot_tn(jnp.concatenate([u[i], v_s[i]], axis=0), bk_end[i]) for i in ch]
    for i in ch:
        y_s = zs[i][c2:2 * c2] + av[i][c2:2 * c2] + yu[i]
        y_ref, cols = out[i]
        y_ref[:, cols] = y_s[0:c] + y_s[c:c2]
        s_ref[i] = s0[i] * dec_end[i] + ds[i]


def _wkv(r, v, nkk, kd, bd, lw, n_batch, seq, tc):
    n, aw = r.shape
    c = CHUNK
    nch = seq // c
    ncc = tc // c

    def fwd_blk(b, i):
        return b * nch + i

    def bwd_blk(b, i):
        return b * nch + jnp.where(i < ncc, ncc - 1 - i, nch - 1 - (i - ncc))

    def specs(blk, d):
        shared = pl.BlockSpec((c, aw), lambda b, i: (blk(b, i), 0))
        perdir = pl.BlockSpec((None, c, aw), lambda b, i: (d, blk(b, i), 0))
        return [shared, shared, shared, perdir, perdir, perdir], shared

    in_f, out_f = specs(fwd_blk, 0)
    in_b, out_b = specs(bwd_blk, 1)
    y = jax.ShapeDtypeStruct((n, aw), F32)
    return pl.pallas_call(
        _wkv_kernel,
        grid=(n_batch, nch),
        in_specs=in_f + in_b,
        out_specs=[out_f, out_b],
        out_shape=[y, y],
        scratch_shapes=[pltpu.VMEM((2 * (aw // PAIR), PAIR, PAIR), F32)],
        compiler_params=_cparams(("parallel", "arbitrary")),
        name="wkv_scan",
    )(r, v, nkk, kd, bd, lw, r, v, nkk, kd, bd, lw)


def _rwkv_out_kernel(yf_ref, yb_ref, r_ref, v_ref, kd_ref, zs_ref, rk_ref, lw_ref, lb_ref, gup_ref, e_ref, o_ref):
    e = e_ref[...]
    inv = 1.0 / A_HEAD
    y = yf_ref[...] + yb_ref[...]
    mu = _head_sum(y, e) * inv
    yc = y - mu
    var = _head_sum(yc * yc, e) * inv
    yh = yc * lax.rsqrt(var + LNX_EPS) * lw_ref[...] + lb_ref[...]
    bonus = _head_sum(r_ref[...] * (kd_ref[0] + kd_ref[1]) * rk_ref[...], e) * v_ref[...]
    zg = zs_ref[...][:, 2 * 2 * A_RANK:2 * 2 * A_RANK + A_GATE_RANK]
    g = _dot_hl(jax.nn.sigmoid(zg), gup_ref[...])
    o_ref[...] = ((yh + bonus) * g).astype(o_ref.dtype)


def _rwkv_output(yf, yb, r, v, kd, z_small, r_k, lnx_w, lnx_b, gate_up, e_head):
    n, aw = r.shape
    rt = RB
    rowspec = pl.BlockSpec((rt, aw), lambda i: (i, 0))
    dspec = pl.BlockSpec((2, rt, aw), lambda i: (0, i, 0))
    vec = pl.BlockSpec((1, aw), lambda i: (0, 0))
    return pl.pallas_call(
        _rwkv_out_kernel,
        grid=(n // rt,),
        in_specs=[
            rowspec, rowspec, rowspec, rowspec, dspec,
            pl.BlockSpec((rt, z_small.shape[1]), lambda i: (i, 0)),
            vec, vec, vec,
            pl.BlockSpec(gate_up.shape, lambda i: (0, 0)),
            pl.BlockSpec(e_head.shape, lambda i: (0, 0)),
        ],
        out_specs=rowspec,
        out_shape=jax.ShapeDtypeStruct((n, aw), BF16),
        compiler_params=_cparams(("parallel",)),
        name="rwkv_output",
    )(yf, yb, r, v, kd, z_small, r_k.reshape(1, aw), lnx_w.reshape(1, aw), lnx_b.reshape(1, aw),
      gate_up, e_head)


def _pool_kernel(u_ref, w_ref, ps_ref, o_ref, pad_ref, *, tc):
    seq = u_ref.shape[0]
    t_lat = seq - tc
    half_max = max(B_WINDOWS) // 2
    lat0 = half_max + tc + 2 * half_max
    gw = u_ref.shape[1]
    zeros = jnp.zeros((2 * half_max, gw), F32)
    pad_ref[0:half_max, :] = zeros[0:half_max]
    pad_ref[half_max:half_max + tc, :] = u_ref[0:tc, :]
    pad_ref[half_max + tc:lat0, :] = zeros
    pad_ref[lat0:lat0 + t_lat, :] = u_ref[tc:seq, :]
    pad_ref[lat0 + t_lat:lat0 + t_lat + half_max, :] = zeros[0:half_max]

    def segment(start, n, win):
        half = win // 2
        acc = pad_ref[start - half:start - half + n, :]
        for j in range(1 - half, half):
            acc = acc + pad_ref[start + j:start + j + n, :]
        t = lax.broadcasted_iota(jnp.int32, (n, gw), 0)
        cnt = jnp.minimum(t + half, n) - jnp.maximum(t - half, 0)
        return acc / cnt.astype(F32) - pad_ref[start:start + n, :]

    for gi, win in enumerate(B_WINDOWS):
        @pl.when(pl.program_id(1) == gi)
        def _(win=win):
            w = w_ref[0]
            ps = ps_ref[...]
            o_ref[0:tc, :] = (_dot(segment(half_max, tc, win).astype(BF16), w) * ps).astype(o_ref.dtype)
            o_ref[tc:seq, :] = (_dot(segment(lat0, t_lat, win).astype(BF16), w) * ps).astype(o_ref.dtype)


def _pool(z_pool, pool_w, pool_scale, n_batch, seq, tc):
    n, bw = z_pool.shape
    ng = len(B_WINDOWS)
    gw = bw // ng
    pad_rows = seq + 4 * (max(B_WINDOWS) // 2)
    return pl.pallas_call(
        functools.partial(_pool_kernel, tc=tc),
        grid=(n_batch, ng),
        in_specs=[
            pl.BlockSpec((seq, gw), lambda b, g: (b, g)),
            pl.BlockSpec((1, gw, gw), lambda b, g: (g, 0, 0)),
            pl.BlockSpec((1, gw), lambda b, g: (0, g)),
        ],
        out_specs=pl.BlockSpec((seq, gw), lambda b, g: (b, g)),
        out_shape=jax.ShapeDtypeStruct((n, bw), BF16),
        scratch_shapes=[pltpu.VMEM((pad_rows, gw), F32)],
        compiler_params=_cparams(("parallel", "parallel")),
        name="pool_mixer",
    )(z_pool, pool_w, pool_scale.reshape(1, bw))


def _attn_kernel(q_ref, k_ref, v_ref, lam_ref, g_ref, o_ref, *, tc, lam_init):
    lq = lam_ref[...]
    lam = (jnp.exp(jnp.sum(lq[0:1] * lq[1:2], axis=-1, keepdims=True))
           - jnp.exp(jnp.sum(lq[2:3] * lq[3:4], axis=-1, keepdims=True)) + lam_init)
    lane = lax.broadcasted_iota(jnp.int32, (1, C_VHEAD), 1)
    q = q_ref[...]
    zero = jnp.zeros_like(q)
    q0 = jnp.where(lane < C_HEAD, q, zero)
    q1 = jnp.where(lane < C_HEAD, zero, q)
    g = g_ref[...] * (1.0 - lam_init)

    def attend(nk):
        k = k_ref[0:nk, :]

        def probs(qh):
            s = _dot_nt(qh, k)
            e = jnp.exp(s - jnp.max(s, axis=-1, keepdims=True))
            return e, 1.0 / jnp.sum(e, axis=-1, keepdims=True)

        e0, i0 = probs(q0)
        e1, i1 = probs(q1)
        amap = (e0 * i0 - e1 * (lam * i1)).astype(BF16)
        o = _dot(amap, v_ref[0:nk, :])
        o = o * lax.rsqrt(jnp.mean(o * o, axis=-1, keepdims=True) + 1e-5) * g
        o_ref[...] = o.astype(o_ref.dtype)

    @pl.when(pl.program_id(2) == 0)
    def _():
        attend(tc)

    @pl.when(pl.program_id(2) != 0)
    def _():
        attend(k_ref.shape[0])


def _attention(q, k, v, lam_qk, subln_g, lam_init, n_batch, seq, tc):
    n, w = q.shape
    nh = w // C_VHEAD
    nb = seq // RB
    kv = pl.BlockSpec((seq, C_VHEAD), lambda b, h, i: (b, h))
    qo = pl.BlockSpec((RB, C_VHEAD), lambda b, h, i: (b * nb + i, h))
    return pl.pallas_call(
        functools.partial(_attn_kernel, tc=tc, lam_init=lam_init),
        grid=(n_batch, nh, nb),
        in_specs=[
            qo, kv, kv,
            pl.BlockSpec(lam_qk.shape, lambda b, h, i: (0, 0)),
            pl.BlockSpec((1, C_VHEAD), lambda b, h, i: (0, 0)),
        ],
        out_specs=qo,
        out_shape=jax.ShapeDtypeStruct((n, w), BF16),
        compiler_params=_cparams(("parallel", "parallel", "parallel")),
        name="diff_attention",
    )(q, k, v, lam_qk, subln_g.reshape(1, C_VHEAD))


def _rope_tables(n_batch, t_lat, tc):
    n_freq = C_HEAD // 4
    inv = ROPE_BASE ** (-jnp.arange(n_freq, dtype=F32) / n_freq)
    rows = t_lat // GRID_W
    t_row = jnp.repeat(jnp.arange(rows, dtype=F32), GRID_W)
    t_col = jnp.tile(jnp.arange(GRID_W, dtype=F32), rows)
    ang_r = t_row[:, None] * inv
    ang_c = t_col[:, None] * inv
    cos = jnp.concatenate([jnp.cos(ang_r)] * 2 + [jnp.cos(ang_c)] * 2, axis=1)
    sin = jnp.concatenate([-jnp.sin(ang_r), jnp.sin(ang_r), -jnp.sin(ang_c), jnp.sin(ang_c)], axis=1)
    cos = jnp.concatenate([jnp.ones((tc, C_HEAD), F32), cos], axis=0)
    sin = jnp.concatenate([jnp.zeros((tc, C_HEAD), F32), sin], axis=0)
    cos = jnp.tile(jnp.tile(cos, (1, LANES // C_HEAD)), (n_batch, 1))
    sin = jnp.tile(jnp.tile(sin, (1, LANES // C_HEAD)), (n_batch, 1))
    return cos, sin


def kernel(x, c, ctx, c_ctx, w_ada, b_ada, norm_g, w_in, rkv_conv, decay_w0, decay_up, iclr_a0, iclr_up, gate_up, k_k, k_a, r_k, lnx_w, lnx_b, pool_w, pool_scale, lam_qk, subln_g, w_branch, w_out, w_ffn_in, w_ffn_out, final_g):
    n_batch, t_lat, d = x.shape
    tc = ctx.shape[1]
    depth = w_ada.shape[0]
    aw = k_k.shape[1]
    assert tc == RB and t_lat % RB == 0 and t_lat % GRID_W == 0 and aw % PAIR == 0
    seq = tc + t_lat
    nb = seq // RB
    n = n_batch * seq

    xa = jnp.concatenate([ctx, x], axis=1).reshape(n, d)
    mr = -(-(n_batch + 1) // SUBLANES) * SUBLANES
    c_all = jnp.zeros((mr, d), F32).at[:n_batch].set(c).at[n_batch].set(c_ctx)
    mods = _mod_tables(c_all, w_ada, b_ada)
    cos_t, sin_t = _rope_tables(n_batch, t_lat, tc)
    hid = lax.broadcasted_iota(jnp.int32, (MXU_COLS, MXU_COLS), 0) // A_HEAD
    e_head = (hid == hid.T).astype(BF16)

    o_small = 3 * aw
    o_pool = o_small + 2 * 2 * A_RANK + A_GATE_RANK
    o_q = o_pool + aw
    o_k = o_q + aw
    o_v = o_k + aw
    o_mix = o_v + aw

    h = _norm_mod(xa, norm_g[0, 0], mods[0], 0, 1, nb, n_batch)
    for l in range(depth):
        lam_init = 0.8 - 0.6 * math.exp(-0.3 * l)
        mod = mods[l]
        wi = w_in[l].astype(BF16)

        z_rkv = _mm(h, wi[:, 0:o_small], F32)
        z_small = _mm(h, wi[:, o_small:o_pool], F32)
        z_pool = _mm(h, wi[:, o_pool:o_q], F32)
        q = _mm_rope(h, wi[:, o_q:o_k], cos_t, sin_t, C_HEAD ** -0.5, BF16)
        k = _mm_rope(h, wi[:, o_k:o_v], cos_t, sin_t, 1.0, BF16)
        v = _mm(h, wi[:, o_v:o_mix], BF16)
        z_mix = _mm(h, wi[:, o_mix:], F32)

        r, va, nkk, kd, bd, lw = _rwkv_prepare(
            z_rkv, z_small, rkv_conv[l], decay_w0[l], decay_up[l], iclr_a0[l], iclr_up[l],
            k_k[l], k_a[l], e_head, seq, tc)
        yf, yb_dir = _wkv(r, va, nkk, kd, bd, lw, n_batch, seq, tc)
        ya = _rwkv_output(yf, yb_dir, r, va, kd, z_small, r_k[l], lnx_w[l], lnx_b[l], gate_up[l], e_head)
        yb = _pool(z_pool, pool_w[l].astype(BF16), pool_scale[l], n_batch, seq, tc)
        yc = _attention(q, k, v, lam_qk[l], subln_g[l], lam_init, n_batch, seq, tc)

        acc = _merge(ya, yb, yc, w_branch[l].astype(BF16), z_mix)
        xa, h2 = _mm_res_norm(acc, w_out[l].astype(BF16), xa, mod, 2, norm_g[l, 1], mod, 3, 4, nb, n_batch)
        ff = _mm_swiglu(h2, w_ffn_in[l].astype(BF16), BF16)
        if l + 1 < depth:
            xa, h = _mm_res_norm(ff, w_ffn_out[l].astype(BF16), xa, mod, 5,
                                 norm_g[l + 1, 0], mods[l + 1], 0, 1, nb, n_batch)
        else:
            xa = _mm_res(ff, w_ffn_out[l].astype(BF16), xa, mod, 5, nb, n_batch)

    out = _final_norm(xa, final_g, nb, n_batch)
    return out.reshape(n_batch, t_lat, d)
```

```python
import functools
import math

import jax
import jax.numpy as jnp
from jax import lax
from jax.experimental import pallas as pl
from jax.experimental.pallas import tpu as pltpu

F32 = jnp.float32
BF16 = jnp.bfloat16

LANES = 128
SUBLANES = 8
MXU_COLS = 256
VMEM_LIMIT = 56 * 1024 * 1024

A_HEAD = 64
A_RANK = 64
A_GATE_RANK = 128
LNX_EPS = 64e-5
B_WINDOWS = (2, 4, 8, 16)
C_HEAD = 64
C_VHEAD = 2 * C_HEAD
ROPE_BASE = 10000.0
GRID_W = 64
N_MOD = 6
N_BRANCH = 3
RB = 256
CHUNK = 64
PAIR = 2 * A_HEAD
HALO = SUBLANES


def _cparams(sem):
    return pltpu.CompilerParams(dimension_semantics=sem, vmem_limit_bytes=VMEM_LIMIT)


def _pick(n, cands):
    for c in cands:
        if n % c == 0:
            return c
    raise ValueError(f"no tile for {n} in {cands}")


def _mod_row(blk, nb, n_batch):
    return jnp.where(blk % nb == 0, n_batch, blk // nb)


def _split3(x):
    hi = x.astype(BF16)
    r1 = x - hi.astype(F32)
    mid = r1.astype(BF16)
    lo = (r1 - mid.astype(F32)).astype(BF16)
    return hi, mid, lo


def _dot(a, b):
    return jnp.dot(a, b, preferred_element_type=F32)


def _dot_nt(a, b):
    return lax.dot_general(a, b, (((1,), (1,)), ((), ())), preferred_element_type=F32)


def _dot_tn(a, b):
    return lax.dot_general(a, b, (((0,), (0,)), ((), ())), preferred_element_type=F32)


def _head_sum(x, e):
    hi = x.astype(BF16)
    lo = (x - hi.astype(F32)).astype(BF16)
    w = e.shape[0]
    return jnp.concatenate(
        [_dot(hi[:, s:s + w], e) + _dot(lo[:, s:s + w], e) for s in range(0, x.shape[1], w)], axis=1)


def _dot_e3(e, x):
    hi, mid, lo = _split3(x)
    return _dot(e, hi) + _dot(e, mid) + _dot(e, lo)


def _dot_hl(x, w):
    xh = x.astype(BF16)
    xl = (x - xh.astype(F32)).astype(BF16)
    wh = w.astype(BF16)
    wl = (w - wh.astype(F32)).astype(BF16)
    return _dot(xh, wh) + _dot(xh, wl) + _dot(xl, wh)


def _mod_kernel(c_ref, w_ref, b_ref, o_ref):
    c = c_ref[...]
    s = (c * jax.nn.sigmoid(c)).astype(BF16)
    o_ref[0] = _dot(s, w_ref[0].astype(BF16)) + b_ref[0]


def _mod_tables(c_all, w_ada, b_ada):
    nl, d, n6 = w_ada.shape
    mr = c_all.shape[0]
    tn = _pick(n6, (1024, 512, 256, 128))
    return pl.pallas_call(
        _mod_kernel,
        grid=(nl, n6 // tn),
        in_specs=[
            pl.BlockSpec((mr, d), lambda l, j: (0, 0)),
            pl.BlockSpec((1, d, tn), lambda l, j: (l, 0, j)),
            pl.BlockSpec((1, 1, tn), lambda l, j: (l, 0, j)),
        ],
        out_specs=pl.BlockSpec((1, mr, tn), lambda l, j: (l, 0, j)),
        out_shape=jax.ShapeDtypeStruct((nl, mr, n6), F32),
        compiler_params=_cparams(("parallel", "parallel")),
        name="adaln_table",
    )(c_all, w_ada, b_ada.reshape(nl, 1, n6))


def _norm_kernel(x_ref, g_ref, shift_ref, scale_ref, o_ref, *, nb, n_batch):
    row = _mod_row(pl.program_id(0), nb, n_batch)
    x = x_ref[...]
    y = x * lax.rsqrt(jnp.mean(x * x, axis=-1, keepdims=True) + 1e-6) * g_ref[...]
    y = y * (1.0 + scale_ref[pl.ds(row, 1), :]) + shift_ref[pl.ds(row, 1), :]
    o_ref[...] = y.astype(o_ref.dtype)


def _norm_mod(x, g, mod, shift_idx, scale_idx, nb, n_batch):
    n, d = x.shape
    mr = mod.shape[0]
    return pl.pallas_call(
        functools.partial(_norm_kernel, nb=nb, n_batch=n_batch),
        grid=(n // RB,),
        in_specs=[
            pl.BlockSpec((RB, d), lambda i: (i, 0)),
            pl.BlockSpec((1, d), lambda i: (0, 0)),
            pl.BlockSpec((mr, d), lambda i: (0, shift_idx)),
            pl.BlockSpec((mr, d), lambda i: (0, scale_idx)),
        ],
        out_specs=pl.BlockSpec((RB, d), lambda i: (i, 0)),
        out_shape=jax.ShapeDtypeStruct((n, d), BF16),
        compiler_params=_cparams(("parallel",)),
        name="norm_mod",
    )(x, g.reshape(1, d), mod, mod)


def _final_norm_kernel(x_ref, g_ref, o_ref):
    x = x_ref[...]
    o_ref[...] = x * lax.rsqrt(jnp.mean(x * x, axis=-1, keepdims=True) + 1e-6) * g_ref[...]


def _final_norm(x, g, nb, n_batch):
    n, d = x.shape
    nlb = nb - 1
    return pl.pallas_call(
        _final_norm_kernel,
        grid=(n_batch, nlb),
        in_specs=[
            pl.BlockSpec((RB, d), lambda b, i: (b * nb + i + 1, 0)),
            pl.BlockSpec((1, d), lambda b, i: (0, 0)),
        ],
        out_specs=pl.BlockSpec((RB, d), lambda b, i: (b * nlb + i, 0)),
        out_shape=jax.ShapeDtypeStruct((n_batch * nlb * RB, d), F32),
        compiler_params=_cparams(("parallel", "parallel")),
        name="final_norm",
    )(x, g.reshape(1, d))


def _mm_plain_kernel(a_ref, w_ref, o_ref):
    o_ref[...] = _dot(a_ref[...], w_ref[...]).astype(o_ref.dtype)


def _mm_tiles(n, k, m):
    tn = _pick(m, (1024, 512, 384, 256, 128)) if k <= 2048 else _pick(m, (512, 256, 128))
    tm = _pick(n, (1024, 512, 256))
    return tm, tn


def _mm(a, w, out_dtype):
    n, k = a.shape
    m = w.shape[1]
    tm, tn = _mm_tiles(n, k, m)
    return pl.pallas_call(
        _mm_plain_kernel,
        grid=(n // tm, m // tn),
        in_specs=[
            pl.BlockSpec((tm, k), lambda i, j: (i, 0)),
            pl.BlockSpec((k, tn), lambda i, j: (0, j)),
        ],
        out_specs=pl.BlockSpec((tm, tn), lambda i, j: (i, j)),
        out_shape=jax.ShapeDtypeStruct((n, m), out_dtype),
        compiler_params=_cparams(("parallel", "parallel")),
        name="mm_plain",
    )(a, w)


def _mm_rope_kernel(a_ref, w_ref, cos_ref, sin_ref, o_ref, *, scale):
    acc = _dot(a_ref[...], w_ref[...])
    lane = lax.broadcasted_iota(jnp.int32, (1, LANES), 1)
    first_half = (lane % (C_HEAD // 2)) < (C_HEAD // 4)
    cos = cos_ref[...]
    sin = sin_ref[...]
    for s in range(acc.shape[1] // LANES):
        u = acc[:, s * LANES:(s + 1) * LANES]
        partner = jnp.where(first_half,
                            pltpu.roll(u, LANES - C_HEAD // 4, 1),
                            pltpu.roll(u, C_HEAD // 4, 1))
        o_ref[:, s * LANES:(s + 1) * LANES] = ((u * cos + partner * sin) * scale).astype(o_ref.dtype)


def _mm_rope(a, w, cos_t, sin_t, scale, out_dtype):
    n, k = a.shape
    m = w.shape[1]
    tm, tn = _mm_tiles(n, k, m)
    return pl.pallas_call(
        functools.partial(_mm_rope_kernel, scale=scale),
        grid=(n // tm, m // tn),
        in_specs=[
            pl.BlockSpec((tm, k), lambda i, j: (i, 0)),
            pl.BlockSpec((k, tn), lambda i, j: (0, j)),
            pl.BlockSpec((tm, LANES), lambda i, j: (i, 0)),
            pl.BlockSpec((tm, LANES), lambda i, j: (i, 0)),
        ],
        out_specs=pl.BlockSpec((tm, tn), lambda i, j: (i, j)),
        out_shape=jax.ShapeDtypeStruct((n, m), out_dtype),
        compiler_params=_cparams(("parallel", "parallel")),
        name="mm_rope",
    )(a, w, cos_t, sin_t)


def _mm_swiglu_kernel(a_ref, wg_ref, wu_ref, o_ref):
    a = a_ref[...]
    g = _dot(a, wg_ref[...])
    u = _dot(a, wu_ref[...])
    o_ref[...] = (g * jax.nn.sigmoid(g) * u).astype(o_ref.dtype)


def _mm_swiglu(a, w, out_dtype):
    n, k = a.shape
    f = w.shape[1] // 2
    tm, tn = _mm_tiles(n, k, f)
    nj = f // tn
    return pl.pallas_call(
        _mm_swiglu_kernel,
        grid=(n // tm, nj),
        in_specs=[
            pl.BlockSpec((tm, k), lambda i, j: (i, 0)),
            pl.BlockSpec((k, tn), lambda i, j: (0, j)),
            pl.BlockSpec((k, tn), lambda i, j: (0, j + nj)),
        ],
        out_specs=pl.BlockSpec((tm, tn), lambda i, j: (i, j)),
        out_shape=jax.ShapeDtypeStruct((n, f), out_dtype),
        compiler_params=_cparams(("parallel", "parallel")),
        name="mm_swiglu",
    )(a, w, w)


def _mm_res_kernel(a_ref, w_ref, res_ref, gate_ref, o_ref, *, nb, n_batch):
    acc = _dot(a_ref[...], w_ref[...])
    tm = acc.shape[0]
    for s in range(tm // RB):
        row = _mod_row(pl.program_id(0) * (tm // RB) + s, nb, n_batch)
        gate = gate_ref[pl.ds(row, 1), :]
        rows = slice(s * RB, (s + 1) * RB)
        o_ref[rows, :] = res_ref[rows, :] + gate * acc[rows, :]


def _mm_res(a, w, res, mod, gate_idx, nb, n_batch):
    n, k = a.shape
    m = w.shape[1]
    mr = mod.shape[0]
    tm, tn = _mm_tiles(n, k, m)
    goff = gate_idx * (m // tn)
    return pl.pallas_call(
        functools.partial(_mm_res_kernel, nb=nb, n_batch=n_batch),
        grid=(n // tm, m // tn),
        in_specs=[
            pl.BlockSpec((tm, k), lambda i, j: (i, 0)),
            pl.BlockSpec((k, tn), lambda i, j: (0, j)),
            pl.BlockSpec((tm, tn), lambda i, j: (i, j)),
            pl.BlockSpec((mr, tn), lambda i, j: (0, goff + j)),
        ],
        out_specs=pl.BlockSpec((tm, tn), lambda i, j: (i, j)),
        out_shape=jax.ShapeDtypeStruct((n, m), F32),
        compiler_params=_cparams(("parallel", "parallel")),
        name="mm_residual",
    )(a, w, res, mod)


def _mm_res_norm_kernel(a_ref, w_ref, res_ref, gate_ref, g_ref, shift_ref, scale_ref, x_ref, h_ref, *, nb, n_batch):
    acc = _dot(a_ref[...], w_ref[...])
    tm = acc.shape[0]
    for s in range(tm // RB):
        row = _mod_row(pl.program_id(0) * (tm // RB) + s, nb, n_batch)
        rows = slice(s * RB, (s + 1) * RB)
        x = res_ref[rows, :] + gate_ref[pl.ds(row, 1), :] * acc[rows, :]
        x_ref[rows, :] = x
        y = x * lax.rsqrt(jnp.mean(x * x, axis=-1, keepdims=True) + 1e-6) * g_ref[...]
        h_ref[rows, :] = (y * (1.0 + scale_ref[pl.ds(row, 1), :]) + shift_ref[pl.ds(row, 1), :]).astype(h_ref.dtype)


def _mm_res_norm(a, w, res, mod, gate_idx, g_next, mod_next, shift_idx, scale_idx, nb, n_batch):
    n, k = a.shape
    d = w.shape[1]
    mr = mod.shape[0]
    tm = _pick(n, (512, 256)) if k <= 2048 else RB
    row = lambda i: (i, 0)
    return pl.pallas_call(
        functools.partial(_mm_res_norm_kernel, nb=nb, n_batch=n_batch),
        grid=(n // tm,),
        in_specs=[
            pl.BlockSpec((tm, k), row),
            pl.BlockSpec((k, d), lambda i: (0, 0), pipeline_mode=pl.Buffered(1)),
            pl.BlockSpec((tm, d), row),
            pl.BlockSpec((mr, d), lambda i: (0, gate_idx)),
            pl.BlockSpec((1, d), lambda i: (0, 0)),
            pl.BlockSpec((mr, d), lambda i: (0, shift_idx)),
            pl.BlockSpec((mr, d), lambda i: (0, scale_idx)),
        ],
        out_specs=[pl.BlockSpec((tm, d), row), pl.BlockSpec((tm, d), row)],
        out_shape=[jax.ShapeDtypeStruct((n, d), F32), jax.ShapeDtypeStruct((n, d), BF16)],
        compiler_params=_cparams(("parallel",)),
        name="mm_residual_norm",
    )(a, w, res, mod, g_next.reshape(1, d), mod_next, mod_next)


def _merge_kernel(ya_ref, yb_ref, yc_ref, w_ref, za_ref, zb_ref, zc_ref, o_ref):
    acc = jax.nn.sigmoid(za_ref[...]) * _dot(ya_ref[...], w_ref[0])
    acc += jax.nn.sigmoid(zb_ref[...]) * _dot(yb_ref[...], w_ref[1])
    acc += jax.nn.sigmoid(zc_ref[...]) * _dot(yc_ref[...], w_ref[2])
    o_ref[...] = acc.astype(o_ref.dtype)


def _merge(ya, yb, yc, w_branch, z_mix):
    n, kw = ya.shape
    d = w_branch.shape[2]
    tm = _pick(n, (1024, 512, 256))
    tn = _pick(d, (512, 256, 128))
    nj = d // tn
    yspec = pl.BlockSpec((tm, kw), lambda i, j: (i, 0))
    return pl.pallas_call(
        _merge_kernel,
        grid=(n // tm, nj),
        in_specs=[
            yspec, yspec, yspec,
            pl.BlockSpec((N_BRANCH, kw, tn), lambda i, j: (0, 0, j)),
            pl.BlockSpec((tm, tn), lambda i, j: (i, j)),
            pl.BlockSpec((tm, tn), lambda i, j: (i, nj + j)),
            pl.BlockSpec((tm, tn), lambda i, j: (i, 2 * nj + j)),
        ],
        out_specs=pl.BlockSpec((tm, tn), lambda i, j: (i, j)),
        out_shape=jax.ShapeDtypeStruct((n, d), BF16),
        compiler_params=_cparams(("parallel", "parallel")),
        name="merge",
    )(ya, yb, yc, w_branch, z_mix, z_mix, z_mix)


def _prep_kernel(zr_ref, zp_ref, zn_ref, zs_ref, conv_ref, w0_ref, wup_ref, a0_ref, aup_ref,
                 kk_ref, ka_ref, e_ref,
                 r_ref, v_ref, nkk_ref, kd_ref, bd_ref, lw_ref, up_ref, dn_ref, *, seq, tc):
    rt = zr_ref.shape[0]
    aw = r_ref.shape[1]
    pos = (pl.program_id(0) * rt) % seq
    prev_ok = jnp.logical_and(pos != 0, pos != tc)
    end = pos + rt
    next_ok = jnp.logical_and(end != seq, end != tc)
    u = zr_ref[...]
    up_ref[...] = pltpu.roll(u, 1, 0)
    up_ref[0:1, :] = jnp.where(prev_ok, zp_ref[HALO - 1:HALO, :], 0.0)
    dn_ref[...] = pltpu.roll(u, rt - 1, 0)
    dn_ref[rt - 1:rt, :] = jnp.where(next_ok, zn_ref[0:1, :], 0.0)
    c = up_ref[...] * conv_ref[0:1, :] + u * conv_ref[1:2, :] + dn_ref[...] * conv_ref[2:3, :]
    r = c[:, 0:aw]
    k = c[:, aw:2 * aw]
    v = c[:, 2 * aw:3 * aw]
    r_ref[...] = r
    v_ref[...] = v
    kx = k * kk_ref[...]
    kk = kx * lax.rsqrt(jnp.maximum(_head_sum(kx * kx, e_ref[...]), 1e-24))
    nkk_ref[...] = -kk
    zs = zs_ref[...]
    ka = ka_ref[...]
    for d in range(2):
        zd = zs[:, d * A_RANK:(d + 1) * A_RANK]
        w_raw = w0_ref[d:d + 1, :] + _dot_hl(jnp.tanh(zd), wup_ref[d])
        lw_ref[d] = -math.exp(-0.5) * jax.nn.sigmoid(w_raw)
        za = zs[:, 2 * A_RANK + d * A_RANK:2 * A_RANK + (d + 1) * A_RANK]
        a = jax.nn.sigmoid(a0_ref[d:d + 1, :] + _dot_hl(za, aup_ref[d]))
        kd_ref[d] = k * (1.0 + (a - 1.0) * ka)
        bd_ref[d] = kk * a


def _rwkv_prepare(z_rkv, z_small, conv_w, w0, w_up, a0, a_up, k_k, k_a, e_head, seq, tc):
    n, w3 = z_rkv.shape
    aw = w3 // 3
    rt = 128
    nh = rt // HALO
    last = n // HALO - 1
    full2 = lambda i: (0, 0)
    full3 = lambda i: (0, 0, 0)
    row = jax.ShapeDtypeStruct((n, aw), F32)
    drow = jax.ShapeDtypeStruct((2, n, aw), F32)
    return pl.pallas_call(
        functools.partial(_prep_kernel, seq=seq, tc=tc),
        grid=(n // rt,),
        in_specs=[
            pl.BlockSpec((rt, w3), lambda i: (i, 0)),
            pl.BlockSpec((HALO, w3), lambda i: (jnp.maximum(i * nh - 1, 0), 0)),
            pl.BlockSpec((HALO, w3), lambda i: (jnp.minimum((i + 1) * nh, last), 0)),
            pl.BlockSpec((rt, z_small.shape[1]), lambda i: (i, 0)),
            pl.BlockSpec(conv_w.shape, full2),
            pl.BlockSpec(w0.shape, full2),
            pl.BlockSpec(w_up.shape, full3),
            pl.BlockSpec(a0.shape, full2),
            pl.BlockSpec(a_up.shape, full3),
            pl.BlockSpec((1, aw), full2),
            pl.BlockSpec((1, aw), full2),
            pl.BlockSpec(e_head.shape, full2),
        ],
        out_specs=[
            pl.BlockSpec((rt, aw), lambda i: (i, 0)),
            pl.BlockSpec((rt, aw), lambda i: (i, 0)),
            pl.BlockSpec((rt, aw), lambda i: (i, 0)),
            pl.BlockSpec((2, rt, aw), lambda i: (0, i, 0)),
            pl.BlockSpec((2, rt, aw), lambda i: (0, i, 0)),
            pl.BlockSpec((2, rt, aw), lambda i: (0, i, 0)),
        ],
        out_shape=[row, row, row, drow, drow, drow],
        scratch_shapes=[pltpu.VMEM((rt, w3), F32), pltpu.VMEM((rt, w3), F32)],
        compiler_params=_cparams(("parallel",)),
        name="rwkv_prepare",
    )(z_rkv, z_rkv, z_rkv, z_small, conv_w, w0, w_up, a0, a_up,
      k_k.reshape(1, aw), k_a.reshape(1, aw), e_head)


def _wkv_kernel(rf_ref, vf_ref, af_ref, kf_ref, bf_ref, lwf_ref, rb_ref, vb_ref, ab_ref, kb_ref, bb_ref, lwb_ref,
                yf_ref, yb_ref, s_ref):
    c = rf_ref.shape[0]
    c2 = 2 * c
    n_pair = rf_ref.shape[1] // PAIR

    @pl.when(pl.program_id(1) == 0)
    def _():
        s_ref[...] = jnp.zeros_like(s_ref)

    ti = lax.broadcasted_iota(jnp.int32, (c, c), 0)
    si = lax.broadcasted_iota(jnp.int32, (c, c), 1)
    i2 = lax.broadcasted_iota(jnp.int32, (c2, c2), 0)
    j2 = lax.broadcasted_iota(jnp.int32, (c2, c2), 1)
    col_minus_row = j2 % c - i2 % c
    xr = i2 ^ j2
    eye = (i2 == j2).astype(F32)
    lane = lax.broadcasted_iota(jnp.int32, (1, PAIR), 1)
    m0 = (lane < A_HEAD).astype(F32)
    m1 = 1.0 - m0

    def stack(x):
        return jnp.concatenate([x * m0, x * m1], axis=0)

    zl, zr, v_s, bk_end, dec_end, strict, incl, out = [], [], [], [], [], [], [], []
    for r_ref, v_ref, a_ref, k_ref, b_ref, lw_ref, y_ref, sgn in (
            (rf_ref, vf_ref, af_ref, kf_ref, bf_ref, lwf_ref, yf_ref, 1),
            (rb_ref, vb_ref, ab_ref, kb_ref, bb_ref, lwb_ref, yb_ref, -1)):
        cum = jnp.where((si - ti) * sgn <= 0, 1.0, 0.0).astype(BF16)
        cs_all = _dot_e3(cum, lw_ref[...])
        last = c - 1 if sgn > 0 else 0
        for p in range(n_pair):
            cols = slice(p * PAIR, (p + 1) * PAIR)
            lw = lw_ref[:, cols]
            cs = cs_all[:, cols]
            cs_end = cs[last:last + 1, :]
            g_inv = jnp.exp(-cs)
            g_end = jnp.exp(cs_end - cs)
            kd = k_ref[:, cols]
            bd = b_ref[:, cols]
            a_t = stack(a_ref[:, cols] * jnp.exp(cs - lw))
            r_t = stack(r_ref[:, cols] * jnp.exp(cs))
            zl.append(jnp.concatenate([a_t, r_t], axis=0).astype(BF16))
            zr.append(jnp.concatenate([stack(bd * g_inv), stack(kd * g_inv)], axis=0).astype(BF16))
            v_s.append(stack(v_ref[:, cols]).astype(BF16))
            bk_end.append(jnp.concatenate([stack(bd * g_end), stack(kd * g_end)], axis=0).astype(BF16))
            dec_end.append(jnp.exp(cs_end))
            strict.append(col_minus_row * sgn < 0)
            incl.append(col_minus_row * sgn <= 0)
            out.append((y_ref, cols))

    ch = range(len(zl))
    aa = [_dot_nt(zl[i], zr[i]) for i in ch]
    s0 = [s_ref[i] for i in ch]
    zs = [_dot_nt(zl[i], s0[i].astype(BF16)) for i in ch]
    a_ab = [jnp.where(strict[i], aa[i][0:c2, 0:c2], 0.0) for i in ch]
    av = [_dot(jnp.concatenate([jnp.where(strict[i], aa[i][0:c2, c2:2 * c2], 0.0),
                                jnp.where(incl[i], aa[i][c2:2 * c2, c2:2 * c2], 0.0)], axis=0).astype(BF16),
               v_s[i]) for i in ch]
    a_rb = [jnp.where(incl[i], aa[i][c2:2 * c2, 0:c2], 0.0).astype(BF16) for i in ch]

    n1 = [jnp.where(xr < 8, a_ab[i], 0.0).astype(BF16) for i in ch]
    n2 = [_dot(n1[i], n1[i]).astype(BF16) for i in ch]
    t = [eye + n1[i].astype(F32) for i in ch]
    t = [t[i] + _dot(t[i].astype(BF16), n2[i]) for i in ch]
    n4 = [_dot(n2[i], n2[i]).astype(BF16) for i in ch]
    t = [t[i] + _dot(t[i].astype(BF16), n4[i]) for i in ch]
    m = 8
    while m < c:
        level = jnp.logical_and(xr >= m, xr < 2 * m)
        tb = [t[i].astype(BF16) for i in ch]
        x = [_dot(jnp.where(level, a_ab[i], 0.0).astype(BF16), tb[i]).astype(BF16) for i in ch]
        t = [t[i] + _dot(tb[i], x[i]) for i in ch]
        m *= 2

    u = [_dot(t[i].astype(BF16), (zs[i][0:c2] + av[i][0:c2]).astype(BF16)).astype(BF16) for i in ch]
    yu = [_dot(a_rb[i], u[i]) for i in ch]
    ds = [_dot_tn(jnp.concatenate([u[i], v_s[i]], axis=0), bk_end[i]) for i in ch]
    for i in ch:
        y_s = zs[i][c2:2 * c2] + av[i][c2:2 * c2] + yu[i]
        y_ref, cols = out[i]
        y_ref[:, cols] = y_s[0:c] + y_s[c:c2]
        s_ref[i] = s0[i] * dec_end[i] + ds[i]


def _wkv(r, v, nkk, kd, bd, lw, n_batch, seq, tc):
    n, aw = r.shape
    c = CHUNK
    nch = seq // c
    ncc = tc // c

    def fwd_blk(b, i):
        return b * nch + i

    def bwd_blk(b, i):
        return b * nch + jnp.where(i < ncc, ncc - 1 - i, nch - 1 - (i - ncc))

    def specs(blk, d):
        shared = pl.BlockSpec((c, aw), lambda b, i: (blk(b, i), 0))
        perdir = pl.BlockSpec((None, c, aw), lambda b, i: (d, blk(b, i), 0))
        return [shared, shared, shared, perdir, perdir, perdir], shared

    in_f, out_f = specs(fwd_blk, 0)
    in_b, out_b = specs(bwd_blk, 1)
    y = jax.ShapeDtypeStruct((n, aw), F32)
    return pl.pallas_call(
        _wkv_kernel,
        grid=(n_batch, nch),
        in_specs=in_f + in_b,
        out_specs=[out_f, out_b],
        out_shape=[y, y],
        scratch_shapes=[pltpu.VMEM((2 * (aw // PAIR), PAIR, PAIR), F32)],
        compiler_params=_cparams(("parallel", "arbitrary")),
        name="wkv_scan",
    )(r, v, nkk, kd, bd, lw, r, v, nkk, kd, bd, lw)


def _rwkv_out_kernel(yf_ref, yb_ref, r_ref, v_ref, kd_ref, zs_ref, rk_ref, lw_ref, lb_ref, gup_ref, e_ref, o_ref):
    e = e_ref[...]
    inv = 1.0 / A_HEAD
    y = yf_ref[...] + yb_ref[...]
    mu = _head_sum(y, e) * inv
    yc = y - mu
    var = _head_sum(yc * yc, e) * inv
    yh = yc * lax.rsqrt(var + LNX_EPS) * lw_ref[...] + lb_ref[...]
    bonus = _head_sum(r_ref[...] * (kd_ref[0] + kd_ref[1]) * rk_ref[...], e) * v_ref[...]
    zg = zs_ref[...][:, 2 * 2 * A_RANK:2 * 2 * A_RANK + A_GATE_RANK]
    g = _dot_hl(jax.nn.sigmoid(zg), gup_ref[...])
    o_ref[...] = ((yh + bonus) * g).astype(o_ref.dtype)


def _rwkv_output(yf, yb, r, v, kd, z_small, r_k, lnx_w, lnx_b, gate_up, e_head):
    n, aw = r.shape
    rt = RB
    rowspec = pl.BlockSpec((rt, aw), lambda i: (i, 0))
    dspec = pl.BlockSpec((2, rt, aw), lambda i: (0, i, 0))
    vec = pl.BlockSpec((1, aw), lambda i: (0, 0))
    return pl.pallas_call(
        _rwkv_out_kernel,
        grid=(n // rt,),
        in_specs=[
            rowspec, rowspec, rowspec, rowspec, dspec,
            pl.BlockSpec((rt, z_small.shape[1]), lambda i: (i, 0)),
            vec, vec, vec,
            pl.BlockSpec(gate_up.shape, lambda i: (0, 0)),
            pl.BlockSpec(e_head.shape, lambda i: (0, 0)),
        ],
        out_specs=rowspec,
        out_shape=jax.ShapeDtypeStruct((n, aw), BF16),
        compiler_params=_cparams(("parallel",)),
        name="rwkv_output",
    )(yf, yb, r, v, kd, z_small, r_k.reshape(1, aw), lnx_w.reshape(1, aw), lnx_b.reshape(1, aw),
      gate_up, e_head)


def _pool_kernel(u_ref, w_ref, ps_ref, o_ref, pad_ref, *, tc):
    seq = u_ref.shape[0]
    t_lat = seq - tc
    half_max = max(B_WINDOWS) // 2
    lat0 = half_max + tc + 2 * half_max
    gw = u_ref.shape[1]
    zeros = jnp.zeros((2 * half_max, gw), F32)
    pad_ref[0:half_max, :] = zeros[0:half_max]
    pad_ref[half_max:half_max + tc, :] = u_ref[0:tc, :]
    pad_ref[half_max + tc:lat0, :] = zeros
    pad_ref[lat0:lat0 + t_lat, :] = u_ref[tc:seq, :]
    pad_ref[lat0 + t_lat:lat0 + t_lat + half_max, :] = zeros[0:half_max]

    def segment(start, n, win):
        half = win // 2
        acc = pad_ref[start - half:start - half + n, :]
        for j in range(1 - half, half):
            acc = acc + pad_ref[start + j:start + j + n, :]
        t = lax.broadcasted_iota(jnp.int32, (n, gw), 0)
        cnt = jnp.minimum(t + half, n) - jnp.maximum(t - half, 0)
        return acc / cnt.astype(F32) - pad_ref[start:start + n, :]

    for gi, win in enumerate(B_WINDOWS):
        @pl.when(pl.program_id(1) == gi)
        def _(win=win):
            w = w_ref[0]
            ps = ps_ref[...]
            o_ref[0:tc, :] = (_dot(segment(half_max, tc, win).astype(BF16), w) * ps).astype(o_ref.dtype)
            o_ref[tc:seq, :] = (_dot(segment(lat0, t_lat, win).astype(BF16), w) * ps).astype(o_ref.dtype)


def _pool(z_pool, pool_w, pool_scale, n_batch, seq, tc):
    n, bw = z_pool.shape
    ng = len(B_WINDOWS)
    gw = bw // ng
    pad_rows = seq + 4 * (max(B_WINDOWS) // 2)
    return pl.pallas_call(
        functools.partial(_pool_kernel, tc=tc),
        grid=(n_batch, ng),
        in_specs=[
            pl.BlockSpec((seq, gw), lambda b, g: (b, g)),
            pl.BlockSpec((1, gw, gw), lambda b, g: (g, 0, 0)),
            pl.BlockSpec((1, gw), lambda b, g: (0, g)),
        ],
        out_specs=pl.BlockSpec((seq, gw), lambda b, g: (b, g)),
        out_shape=jax.ShapeDtypeStruct((n, bw), BF16),
        scratch_shapes=[pltpu.VMEM((pad_rows, gw), F32)],
        compiler_params=_cparams(("parallel", "parallel")),
        name="pool_mixer",
    )(z_pool, pool_w, pool_scale.reshape(1, bw))


def _attn_kernel(q_ref, k_ref, v_ref, lam_ref, g_ref, o_ref, *, tc, lam_init):
    lq = lam_ref[...]
    lam = (jnp.exp(jnp.sum(lq[0:1] * lq[1:2], axis=-1, keepdims=True))
           - jnp.exp(jnp.sum(lq[2:3] * lq[3:4], axis=-1, keepdims=True)) + lam_init)
    lane = lax.broadcasted_iota(jnp.int32, (1, C_VHEAD), 1)
    q = q_ref[...]
    zero = jnp.zeros_like(q)
    q0 = jnp.where(lane < C_HEAD, q, zero)
    q1 = jnp.where(lane < C_HEAD, zero, q)
    g = g_ref[...] * (1.0 - lam_init)

    def attend(nk):
        k = k_ref[0:nk, :]

        def probs(qh):
            s = _dot_nt(qh, k)
            e = jnp.exp(s - jnp.max(s, axis=-1, keepdims=True))
            return e, 1.0 / jnp.sum(e, axis=-1, keepdims=True)

        e0, i0 = probs(q0)
        e1, i1 = probs(q1)
        amap = (e0 * i0 - e1 * (lam * i1)).astype(BF16)
        o = _dot(amap, v_ref[0:nk, :])
        o = o * lax.rsqrt(jnp.mean(o * o, axis=-1, keepdims=True) + 1e-5) * g
        o_ref[...] = o.astype(o_ref.dtype)

    @pl.when(pl.program_id(2) == 0)
    def _():
        attend(tc)

    @pl.when(pl.program_id(2) != 0)
    def _():
        attend(k_ref.shape[0])


def _attention(q, k, v, lam_qk, subln_g, lam_init, n_batch, seq, tc):
    n, w = q.shape
    nh = w // C_VHEAD
    nb = seq // RB
    kv = pl.BlockSpec((seq, C_VHEAD), lambda b, h, i: (b, h))
    qo = pl.BlockSpec((RB, C_VHEAD), lambda b, h, i: (b * nb + i, h))
    return pl.pallas_call(
        functools.partial(_attn_kernel, tc=tc, lam_init=lam_init),
        grid=(n_batch, nh, nb),
        in_specs=[
            qo, kv, kv,
            pl.BlockSpec(lam_qk.shape, lambda b, h, i: (0, 0)),
            pl.BlockSpec((1, C_VHEAD), lambda b, h, i: (0, 0)),
        ],
        out_specs=qo,
        out_shape=jax.ShapeDtypeStruct((n, w), BF16),
        compiler_params=_cparams(("parallel", "parallel", "parallel")),
        name="diff_attention",
    )(q, k, v, lam_qk, subln_g.reshape(1, C_VHEAD))


def _rope_tables(n_batch, t_lat, tc):
    n_freq = C_HEAD // 4
    inv = ROPE_BASE ** (-jnp.arange(n_freq, dtype=F32) / n_freq)
    rows = t_lat // GRID_W
    t_row = jnp.repeat(jnp.arange(rows, dtype=F32), GRID_W)
    t_col = jnp.tile(jnp.arange(GRID_W, dtype=F32), rows)
    ang_r = t_row[:, None] * inv
    ang_c = t_col[:, None] * inv
    cos = jnp.concatenate([jnp.cos(ang_r)] * 2 + [jnp.cos(ang_c)] * 2, axis=1)
    sin = jnp.concatenate([-jnp.sin(ang_r), jnp.sin(ang_r), -jnp.sin(ang_c), jnp.sin(ang_c)], axis=1)
    cos = jnp.concatenate([jnp.ones((tc, C_HEAD), F32), cos], axis=0)
    sin = jnp.concatenate([jnp.zeros((tc, C_HEAD), F32), sin], axis=0)
    cos = jnp.tile(jnp.tile(cos, (1, LANES // C_HEAD)), (n_batch, 1))
    sin = jnp.tile(jnp.tile(sin, (1, LANES // C_HEAD)), (n_batch, 1))
    return cos, sin


def kernel(x, c, ctx, c_ctx, w_ada, b_ada, norm_g, w_in, rkv_conv, decay_w0, decay_up, iclr_a0, iclr_up, gate_up, k_k, k_a, r_k, lnx_w, lnx_b, pool_w, pool_scale, lam_qk, subln_g, w_branch, w_out, w_ffn_in, w_ffn_out, final_g):
    n_batch, t_lat, d = x.shape
    tc = ctx.shape[1]
    depth = w_ada.shape[0]
    aw = k_k.shape[1]
    assert tc == RB and t_lat % RB == 0 and t_lat % GRID_W == 0 and aw % PAIR == 0
    seq = tc + t_lat
    nb = seq // RB
    n = n_batch * seq

    xa = jnp.concatenate([ctx, x], axis=1).reshape(n, d)
    mr = -(-(n_batch + 1) // SUBLANES) * SUBLANES
    c_all = jnp.zeros((mr, d), F32).at[:n_batch].set(c).at[n_batch].set(c_ctx)
    mods = _mod_tables(c_all, w_ada, b_ada)
    cos_t, sin_t = _rope_tables(n_batch, t_lat, tc)
    hid = lax.broadcasted_iota(jnp.int32, (MXU_COLS, MXU_COLS), 0) // A_HEAD
    e_head = (hid == hid.T).astype(BF16)

    o_small = 3 * aw
    o_pool = o_small + 2 * 2 * A_RANK + A_GATE_RANK
    o_q = o_pool + aw
    o_k = o_q + aw
    o_v = o_k + aw
    o_mix = o_v + aw

    h = _norm_mod(xa, norm_g[0, 0], mods[0], 0, 1, nb, n_batch)
    for l in range(depth):
        lam_init = 0.8 - 0.6 * math.exp(-0.3 * l)
        mod = mods[l]

        def wi(lo, hi):
            return w_in[l, :, lo:hi].astype(BF16)

        z_rkv = _mm(h, wi(0, o_small), F32)
        z_small = _mm(h, wi(o_small, o_pool), F32)
        z_pool = _mm(h, wi(o_pool, o_q), F32)
        q = _mm_rope(h, wi(o_q, o_k), cos_t, sin_t, C_HEAD ** -0.5, BF16)
        k = _mm_rope(h, wi(o_k, o_v), cos_t, sin_t, 1.0, BF16)
        v = _mm(h, wi(o_v, o_mix), BF16)
        z_mix = _mm(h, wi(o_mix, w_in.shape[2]), F32)

        r, va, nkk, kd, bd, lw = _rwkv_prepare(
            z_rkv, z_small, rkv_conv[l], decay_w0[l], decay_up[l], iclr_a0[l], iclr_up[l],
            k_k[l], k_a[l], e_head, seq, tc)
        yf, yb_dir = _wkv(r, va, nkk, kd, bd, lw, n_batch, seq, tc)
        ya = _rwkv_output(yf, yb_dir, r, va, kd, z_small, r_k[l], lnx_w[l], lnx_b[l], gate_up[l], e_head)
        yb = _pool(z_pool, pool_w[l].astype(BF16), pool_scale[l], n_batch, seq, tc)
        yc = _attention(q, k, v, lam_qk[l], subln_g[l], lam_init, n_batch, seq, tc)

        acc = _merge(ya, yb, yc, w_branch[l].astype(BF16), z_mix)
        xa, h2 = _mm_res_norm(acc, w_out[l].astype(BF16), xa, mod, 2, norm_g[l, 1], mod, 3, 4, nb, n_batch)
        ff = _mm_swiglu(h2, w_ffn_in[l].astype(BF16), BF16)
        if l + 1 < depth:
            xa, h = _mm_res_norm(ff, w_ffn_out[l].astype(BF16), xa, mod, 5,
                                 norm_g[l + 1, 0], mods[l + 1], 0, 1, nb, n_batch)
        else:
            xa = _mm_res(ff, w_ffn_out[l].astype(BF16), xa, mod, 5, nb, n_batch)

    out = _final_norm(xa, final_g, nb, n_batch)
    return out.reshape(n_batch, t_lat, d)
```

```python
import functools
import math

import jax
import jax.numpy as jnp
from jax import lax
from jax.experimental import pallas as pl
from jax.experimental.pallas import tpu as pltpu

F32 = jnp.float32
BF16 = jnp.bfloat16

LANES = 128
SUBLANES = 8
MXU_COLS = 256
VMEM_LIMIT = 56 * 1024 * 1024

A_HEAD = 64
A_RANK = 64
A_GATE_RANK = 128
LNX_EPS = 64e-5
B_WINDOWS = (2, 4, 8, 16)
C_HEAD = 64
C_VHEAD = 2 * C_HEAD
ROPE_BASE = 10000.0
GRID_W = 64
N_MOD = 6
N_BRANCH = 3
RB = 256
CHUNK = 64
PAIR = 2 * A_HEAD
HALO = SUBLANES
ATTN_HEADS_PER_STEP = 4


def _cparams(sem):
    return pltpu.CompilerParams(dimension_semantics=sem, vmem_limit_bytes=VMEM_LIMIT)


def _pick(n, cands):
    for c in cands:
        if n % c == 0:
            return c
    raise ValueError(f"no tile for {n} in {cands}")


def _mod_row(blk, nb, n_batch):
    return jnp.where(blk % nb == 0, n_batch, blk // nb)


def _split3(x):
    hi = x.astype(BF16)
    r1 = x - hi.astype(F32)
    mid = r1.astype(BF16)
    lo = (r1 - mid.astype(F32)).astype(BF16)
    return hi, mid, lo


def _dot(a, b):
    return jnp.dot(a, b, preferred_element_type=F32)


def _dot_nt(a, b):
    return lax.dot_general(a, b, (((1,), (1,)), ((), ())), preferred_element_type=F32)


def _dot_tn(a, b):
    return lax.dot_general(a, b, (((0,), (0,)), ((), ())), preferred_element_type=F32)


def _head_sum(x, e):
    hi = x.astype(BF16)
    lo = (x - hi.astype(F32)).astype(BF16)
    w = e.shape[0]
    return jnp.concatenate(
        [_dot(hi[:, s:s + w], e) + _dot(lo[:, s:s + w], e) for s in range(0, x.shape[1], w)], axis=1)


def _dot_e3(e, x):
    hi, mid, lo = _split3(x)
    return _dot(e, hi) + _dot(e, mid) + _dot(e, lo)


def _dot_hl(x, w):
    xh = x.astype(BF16)
    xl = (x - xh.astype(F32)).astype(BF16)
    wh = w.astype(BF16)
    wl = (w - wh.astype(F32)).astype(BF16)
    return _dot(xh, wh) + _dot(xh, wl) + _dot(xl, wh)


def _mod_kernel(c_ref, w_ref, b_ref, o_ref):
    c = c_ref[...]
    s = (c * jax.nn.sigmoid(c)).astype(BF16)
    o_ref[0] = _dot(s, w_ref[0].astype(BF16)) + b_ref[0]


def _mod_tables(c_all, w_ada, b_ada):
    nl, d, n6 = w_ada.shape
    mr = c_all.shape[0]
    tn = _pick(n6, (1024, 512, 256, 128))
    return pl.pallas_call(
        _mod_kernel,
        grid=(nl, n6 // tn),
        in_specs=[
            pl.BlockSpec((mr, d), lambda l, j: (0, 0)),
            pl.BlockSpec((1, d, tn), lambda l, j: (l, 0, j)),
            pl.BlockSpec((1, 1, tn), lambda l, j: (l, 0, j)),
        ],
        out_specs=pl.BlockSpec((1, mr, tn), lambda l, j: (l, 0, j)),
        out_shape=jax.ShapeDtypeStruct((nl, mr, n6), F32),
        compiler_params=_cparams(("parallel", "parallel")),
        name="adaln_table",
    )(c_all, w_ada, b_ada.reshape(nl, 1, n6))


def _norm_kernel(x_ref, g_ref, shift_ref, scale_ref, o_ref, *, nb, n_batch):
    row = _mod_row(pl.program_id(0), nb, n_batch)
    x = x_ref[...]
    y = x * lax.rsqrt(jnp.mean(x * x, axis=-1, keepdims=True) + 1e-6) * g_ref[...]
    y = y * (1.0 + scale_ref[pl.ds(row, 1), :]) + shift_ref[pl.ds(row, 1), :]
    o_ref[...] = y.astype(o_ref.dtype)


def _norm_mod(x, g, mod, shift_idx, scale_idx, nb, n_batch):
    n, d = x.shape
    mr = mod.shape[0]
    return pl.pallas_call(
        functools.partial(_norm_kernel, nb=nb, n_batch=n_batch),
        grid=(n // RB,),
        in_specs=[
            pl.BlockSpec((RB, d), lambda i: (i, 0)),
            pl.BlockSpec((1, d), lambda i: (0, 0)),
            pl.BlockSpec((mr, d), lambda i: (0, shift_idx)),
            pl.BlockSpec((mr, d), lambda i: (0, scale_idx)),
        ],
        out_specs=pl.BlockSpec((RB, d), lambda i: (i, 0)),
        out_shape=jax.ShapeDtypeStruct((n, d), BF16),
        compiler_params=_cparams(("parallel",)),
        name="norm_mod",
    )(x, g.reshape(1, d), mod, mod)


def _final_norm_kernel(x_ref, g_ref, o_ref):
    x = x_ref[...]
    o_ref[...] = x * lax.rsqrt(jnp.mean(x * x, axis=-1, keepdims=True) + 1e-6) * g_ref[...]


def _final_norm(x, g, nb, n_batch):
    n, d = x.shape
    nlb = nb - 1
    return pl.pallas_call(
        _final_norm_kernel,
        grid=(n_batch, nlb),
        in_specs=[
            pl.BlockSpec((RB, d), lambda b, i: (b * nb + i + 1, 0)),
            pl.BlockSpec((1, d), lambda b, i: (0, 0)),
        ],
        out_specs=pl.BlockSpec((RB, d), lambda b, i: (b * nlb + i, 0)),
        out_shape=jax.ShapeDtypeStruct((n_batch * nlb * RB, d), F32),
        compiler_params=_cparams(("parallel", "parallel")),
        name="final_norm",
    )(x, g.reshape(1, d))


def _mm_plain_kernel(a_ref, w_ref, o_ref):
    o_ref[...] = _dot(a_ref[...], w_ref[...]).astype(o_ref.dtype)


def _mm_tiles(n, k, m):
    tn = _pick(m, (1024, 512, 384, 256, 128)) if k <= 2048 else _pick(m, (512, 256, 128))
    tm = _pick(n, (1024, 512, 256))
    return tm, tn


def _mm(a, w, out_dtype):
    n, k = a.shape
    m = w.shape[1]
    tm, tn = _mm_tiles(n, k, m)
    return pl.pallas_call(
        _mm_plain_kernel,
        grid=(n // tm, m // tn),
        in_specs=[
            pl.BlockSpec((tm, k), lambda i, j: (i, 0)),
            pl.BlockSpec((k, tn), lambda i, j: (0, j)),
        ],
        out_specs=pl.BlockSpec((tm, tn), lambda i, j: (i, j)),
        out_shape=jax.ShapeDtypeStruct((n, m), out_dtype),
        compiler_params=_cparams(("parallel", "parallel")),
        name="mm_plain",
    )(a, w)


def _mm_rope_kernel(a_ref, w_ref, cos_ref, sin_ref, o_ref, *, scale):
    acc = _dot(a_ref[...], w_ref[...])
    lane = lax.broadcasted_iota(jnp.int32, (1, LANES), 1)
    first_half = (lane % (C_HEAD // 2)) < (C_HEAD // 4)
    cos = cos_ref[...]
    sin = sin_ref[...]
    for s in range(acc.shape[1] // LANES):
        u = acc[:, s * LANES:(s + 1) * LANES]
        partner = jnp.where(first_half,
                            pltpu.roll(u, LANES - C_HEAD // 4, 1),
                            pltpu.roll(u, C_HEAD // 4, 1))
        o_ref[:, s * LANES:(s + 1) * LANES] = ((u * cos + partner * sin) * scale).astype(o_ref.dtype)


def _mm_rope(a, w, cos_t, sin_t, scale, out_dtype):
    n, k = a.shape
    m = w.shape[1]
    tm, tn = _mm_tiles(n, k, m)
    return pl.pallas_call(
        functools.partial(_mm_rope_kernel, scale=scale),
        grid=(n // tm, m // tn),
        in_specs=[
            pl.BlockSpec((tm, k), lambda i, j: (i, 0)),
            pl.BlockSpec((k, tn), lambda i, j: (0, j)),
            pl.BlockSpec((tm, LANES), lambda i, j: (i, 0)),
            pl.BlockSpec((tm, LANES), lambda i, j: (i, 0)),
        ],
        out_specs=pl.BlockSpec((tm, tn), lambda i, j: (i, j)),
        out_shape=jax.ShapeDtypeStruct((n, m), out_dtype),
        compiler_params=_cparams(("parallel", "parallel")),
        name="mm_rope",
    )(a, w, cos_t, sin_t)


def _mm_swiglu_kernel(a_ref, wg_ref, wu_ref, o_ref):
    a = a_ref[...]
    g = _dot(a, wg_ref[...])
    u = _dot(a, wu_ref[...])
    o_ref[...] = (g * jax.nn.sigmoid(g) * u).astype(o_ref.dtype)


def _mm_swiglu(a, w, out_dtype):
    n, k = a.shape
    f = w.shape[1] // 2
    tm, tn = _mm_tiles(n, k, f)
    nj = f // tn
    return pl.pallas_call(
        _mm_swiglu_kernel,
        grid=(n // tm, nj),
        in_specs=[
            pl.BlockSpec((tm, k), lambda i, j: (i, 0)),
            pl.BlockSpec((k, tn), lambda i, j: (0, j)),
            pl.BlockSpec((k, tn), lambda i, j: (0, j + nj)),
        ],
        out_specs=pl.BlockSpec((tm, tn), lambda i, j: (i, j)),
        out_shape=jax.ShapeDtypeStruct((n, f), out_dtype),
        compiler_params=_cparams(("parallel", "parallel")),
        name="mm_swiglu",
    )(a, w, w)


def _mm_res_kernel(a_ref, w_ref, res_ref, gate_ref, o_ref, *, nb, n_batch):
    acc = _dot(a_ref[...], w_ref[...])
    tm = acc.shape[0]
    for s in range(tm // RB):
        row = _mod_row(pl.program_id(0) * (tm // RB) + s, nb, n_batch)
        gate = gate_ref[pl.ds(row, 1), :]
        rows = slice(s * RB, (s + 1) * RB)
        o_ref[rows, :] = res_ref[rows, :] + gate * acc[rows, :]


def _mm_res(a, w, res, mod, gate_idx, nb, n_batch):
    n, k = a.shape
    m = w.shape[1]
    mr = mod.shape[0]
    tm, tn = _mm_tiles(n, k, m)
    goff = gate_idx * (m // tn)
    return pl.pallas_call(
        functools.partial(_mm_res_kernel, nb=nb, n_batch=n_batch),
        grid=(n // tm, m // tn),
        in_specs=[
            pl.BlockSpec((tm, k), lambda i, j: (i, 0)),
            pl.BlockSpec((k, tn), lambda i, j: (0, j)),
            pl.BlockSpec((tm, tn), lambda i, j: (i, j)),
            pl.BlockSpec((mr, tn), lambda i, j: (0, goff + j)),
        ],
        out_specs=pl.BlockSpec((tm, tn), lambda i, j: (i, j)),
        out_shape=jax.ShapeDtypeStruct((n, m), F32),
        compiler_params=_cparams(("parallel", "parallel")),
        name="mm_residual",
    )(a, w, res, mod)


def _mm_res_norm_kernel(a_ref, w_ref, res_ref, gate_ref, g_ref, shift_ref, scale_ref, x_ref, h_ref, *, nb, n_batch):
    acc = _dot(a_ref[...], w_ref[...])
    tm = acc.shape[0]
    for s in range(tm // RB):
        row = _mod_row(pl.program_id(0) * (tm // RB) + s, nb, n_batch)
        rows = slice(s * RB, (s + 1) * RB)
        x = res_ref[rows, :] + gate_ref[pl.ds(row, 1), :] * acc[rows, :]
        x_ref[rows, :] = x
        y = x * lax.rsqrt(jnp.mean(x * x, axis=-1, keepdims=True) + 1e-6) * g_ref[...]
        h_ref[rows, :] = (y * (1.0 + scale_ref[pl.ds(row, 1), :]) + shift_ref[pl.ds(row, 1), :]).astype(h_ref.dtype)


def _mm_res_norm(a, w, res, mod, gate_idx, g_next, mod_next, shift_idx, scale_idx, nb, n_batch):
    n, k = a.shape
    d = w.shape[1]
    mr = mod.shape[0]
    tm = _pick(n, (512, 256)) if k <= 2048 else RB
    row = lambda i: (i, 0)
    return pl.pallas_call(
        functools.partial(_mm_res_norm_kernel, nb=nb, n_batch=n_batch),
        grid=(n // tm,),
        in_specs=[
            pl.BlockSpec((tm, k), row),
            pl.BlockSpec((k, d), lambda i: (0, 0), pipeline_mode=pl.Buffered(1)),
            pl.BlockSpec((tm, d), row),
            pl.BlockSpec((mr, d), lambda i: (0, gate_idx)),
            pl.BlockSpec((1, d), lambda i: (0, 0)),
            pl.BlockSpec((mr, d), lambda i: (0, shift_idx)),
            pl.BlockSpec((mr, d), lambda i: (0, scale_idx)),
        ],
        out_specs=[pl.BlockSpec((tm, d), row), pl.BlockSpec((tm, d), row)],
        out_shape=[jax.ShapeDtypeStruct((n, d), F32), jax.ShapeDtypeStruct((n, d), BF16)],
        compiler_params=_cparams(("parallel",)),
        name="mm_residual_norm",
    )(a, w, res, mod, g_next.reshape(1, d), mod_next, mod_next)


def _merge_kernel(ya_ref, yb_ref, yc_ref, w_ref, za_ref, zb_ref, zc_ref, o_ref):
    acc = jax.nn.sigmoid(za_ref[...]) * _dot(ya_ref[...], w_ref[0])
    acc += jax.nn.sigmoid(zb_ref[...]) * _dot(yb_ref[...], w_ref[1])
    acc += jax.nn.sigmoid(zc_ref[...]) * _dot(yc_ref[...], w_ref[2])
    o_ref[...] = acc.astype(o_ref.dtype)


def _merge(ya, yb, yc, w_branch, z_mix):
    n, kw = ya.shape
    d = w_branch.shape[2]
    tm = _pick(n, (1024, 512, 256))
    tn = _pick(d, (512, 256, 128))
    nj = d // tn
    yspec = pl.BlockSpec((tm, kw), lambda i, j: (i, 0))
    return pl.pallas_call(
        _merge_kernel,
        grid=(n // tm, nj),
        in_specs=[
            yspec, yspec, yspec,
            pl.BlockSpec((N_BRANCH, kw, tn), lambda i, j: (0, 0, j)),
            pl.BlockSpec((tm, tn), lambda i, j: (i, j)),
            pl.BlockSpec((tm, tn), lambda i, j: (i, nj + j)),
            pl.BlockSpec((tm, tn), lambda i, j: (i, 2 * nj + j)),
        ],
        out_specs=pl.BlockSpec((tm, tn), lambda i, j: (i, j)),
        out_shape=jax.ShapeDtypeStruct((n, d), BF16),
        compiler_params=_cparams(("parallel", "parallel")),
        name="merge",
    )(ya, yb, yc, w_branch, z_mix, z_mix, z_mix)


def _prep_kernel(zr_ref, zp_ref, zn_ref, zs_ref, conv_ref, w0_ref, wup_ref, a0_ref, aup_ref,
                 kk_ref, ka_ref, e_ref,
                 r_ref, v_ref, nkk_ref, kd_ref, bd_ref, lw_ref, up_ref, dn_ref, *, seq, tc):
    rt = zr_ref.shape[0]
    aw = r_ref.shape[1]
    pos = (pl.program_id(0) * rt) % seq
    prev_ok = jnp.logical_and(pos != 0, pos != tc)
    end = pos + rt
    next_ok = jnp.logical_and(end != seq, end != tc)
    u = zr_ref[...]
    up_ref[...] = pltpu.roll(u, 1, 0)
    up_ref[0:1, :] = jnp.where(prev_ok, zp_ref[HALO - 1:HALO, :], 0.0)
    dn_ref[...] = pltpu.roll(u, rt - 1, 0)
    dn_ref[rt - 1:rt, :] = jnp.where(next_ok, zn_ref[0:1, :], 0.0)
    c = up_ref[...] * conv_ref[0:1, :] + u * conv_ref[1:2, :] + dn_ref[...] * conv_ref[2:3, :]
    r = c[:, 0:aw]
    k = c[:, aw:2 * aw]
    v = c[:, 2 * aw:3 * aw]
    r_ref[...] = r
    v_ref[...] = v
    kx = k * kk_ref[...]
    kk = kx * lax.rsqrt(jnp.maximum(_head_sum(kx * kx, e_ref[...]), 1e-24))
    nkk_ref[...] = -kk
    zs = zs_ref[...]
    ka = ka_ref[...]
    for d in range(2):
        zd = zs[:, d * A_RANK:(d + 1) * A_RANK]
        w_raw = w0_ref[d:d + 1, :] + _dot_hl(jnp.tanh(zd), wup_ref[d])
        lw_ref[d] = -math.exp(-0.5) * jax.nn.sigmoid(w_raw)
        za = zs[:, 2 * A_RANK + d * A_RANK:2 * A_RANK + (d + 1) * A_RANK]
        a = jax.nn.sigmoid(a0_ref[d:d + 1, :] + _dot_hl(za, aup_ref[d]))
        kd_ref[d] = k * (1.0 + (a - 1.0) * ka)
        bd_ref[d] = kk * a


def _rwkv_prepare(z_rkv, z_small, conv_w, w0, w_up, a0, a_up, k_k, k_a, e_head, seq, tc):
    n, w3 = z_rkv.shape
    aw = w3 // 3
    rt = 128
    nh = rt // HALO
    last = n // HALO - 1
    full2 = lambda i: (0, 0)
    full3 = lambda i: (0, 0, 0)
    row = jax.ShapeDtypeStruct((n, aw), F32)
    drow = jax.ShapeDtypeStruct((2, n, aw), F32)
    return pl.pallas_call(
        functools.partial(_prep_kernel, seq=seq, tc=tc),
        grid=(n // rt,),
        in_specs=[
            pl.BlockSpec((rt, w3), lambda i: (i, 0)),
            pl.BlockSpec((HALO, w3), lambda i: (jnp.maximum(i * nh - 1, 0), 0)),
            pl.BlockSpec((HALO, w3), lambda i: (jnp.minimum((i + 1) * nh, last), 0)),
            pl.BlockSpec((rt, z_small.shape[1]), lambda i: (i, 0)),
            pl.BlockSpec(conv_w.shape, full2),
            pl.BlockSpec(w0.shape, full2),
            pl.BlockSpec(w_up.shape, full3),
            pl.BlockSpec(a0.shape, full2),
            pl.BlockSpec(a_up.shape, full3),
            pl.BlockSpec((1, aw), full2),
            pl.BlockSpec((1, aw), full2),
            pl.BlockSpec(e_head.shape, full2),
        ],
        out_specs=[
            pl.BlockSpec((rt, aw), lambda i: (i, 0)),
            pl.BlockSpec((rt, aw), lambda i: (i, 0)),
            pl.BlockSpec((rt, aw), lambda i: (i, 0)),
            pl.BlockSpec((2, rt, aw), lambda i: (0, i, 0)),
            pl.BlockSpec((2, rt, aw), lambda i: (0, i, 0)),
            pl.BlockSpec((2, rt, aw), lambda i: (0, i, 0)),
        ],
        out_shape=[row, row, row, drow, drow, drow],
        scratch_shapes=[pltpu.VMEM((rt, w3), F32), pltpu.VMEM((rt, w3), F32)],
        compiler_params=_cparams(("parallel",)),
        name="rwkv_prepare",
    )(z_rkv, z_rkv, z_rkv, z_small, conv_w, w0, w_up, a0, a_up,
      k_k.reshape(1, aw), k_a.reshape(1, aw), e_head)


def _wkv_kernel(rf_ref, vf_ref, af_ref, kf_ref, bf_ref, lwf_ref, rb_ref, vb_ref, ab_ref, kb_ref, bb_ref, lwb_ref,
                yf_ref, yb_ref, s_ref):
    c = rf_ref.shape[0]
    c2 = 2 * c
    n_pair = rf_ref.shape[1] // PAIR

    @pl.when(pl.program_id(1) == 0)
    def _():
        s_ref[...] = jnp.zeros_like(s_ref)

    ti = lax.broadcasted_iota(jnp.int32, (c, c), 0)
    si = lax.broadcasted_iota(jnp.int32, (c, c), 1)
    i2 = lax.broadcasted_iota(jnp.int32, (c2, c2), 0)
    j2 = lax.broadcasted_iota(jnp.int32, (c2, c2), 1)
    col_minus_row = j2 % c - i2 % c
    xr = i2 ^ j2
    eye = (i2 == j2).astype(F32)
    lane = lax.broadcasted_iota(jnp.int32, (1, PAIR), 1)
    m0 = (lane < A_HEAD).astype(F32)
    m1 = 1.0 - m0

    def stack(x):
        return jnp.concatenate([x * m0, x * m1], axis=0)

    zl, zr, v_s, bk_end, dec_end, strict, incl, out = [], [], [], [], [], [], [], []
    for r_ref, v_ref, a_ref, k_ref, b_ref, lw_ref, y_ref, sgn in (
            (rf_ref, vf_ref, af_ref, kf_ref, bf_ref, lwf_ref, yf_ref, 1),
            (rb_ref, vb_ref, ab_ref, kb_ref, bb_ref, lwb_ref, yb_ref, -1)):
        cum = jnp.where((si - ti) * sgn <= 0, 1.0, 0.0).astype(BF16)
        cs_all = _dot_e3(cum, lw_ref[...])
        last = c - 1 if sgn > 0 else 0
        for p in range(n_pair):
            cols = slice(p * PAIR, (p + 1) * PAIR)
            lw = lw_ref[:, cols]
            cs = cs_all[:, cols]
            cs_end = cs[last:last + 1, :]
            g_inv = jnp.exp(-cs)
            g_end = jnp.exp(cs_end - cs)
            kd = k_ref[:, cols]
            bd = b_ref[:, cols]
            a_t = stack(a_ref[:, cols] * jnp.exp(cs - lw))
            r_t = stack(r_ref[:, cols] * jnp.exp(cs))
            zl.append(jnp.concatenate([a_t, r_t], axis=0).astype(BF16))
            zr.append(jnp.concatenate([stack(bd * g_inv), stack(kd * g_inv)], axis=0).astype(BF16))
            v_s.append(stack(v_ref[:, cols]).astype(BF16))
            bk_end.append(jnp.concatenate([stack(bd * g_end), stack(kd * g_end)], axis=0).astype(BF16))
            dec_end.append(jnp.exp(cs_end))
            strict.append(col_minus_row * sgn < 0)
            incl.append(col_minus_row * sgn <= 0)
            out.append((y_ref, cols))

    ch = range(len(zl))
    aa = [_dot_nt(zl[i], zr[i]) for i in ch]
    s0 = [s_ref[i] for i in ch]
    zs = [_dot_nt(zl[i], s0[i].astype(BF16)) for i in ch]
    a_ab = [jnp.where(strict[i], aa[i][0:c2, 0:c2], 0.0) for i in ch]
    av = [_dot(jnp.concatenate([jnp.where(strict[i], aa[i][0:c2, c2:2 * c2], 0.0),
                                jnp.where(incl[i], aa[i][c2:2 * c2, c2:2 * c2], 0.0)], axis=0).astype(BF16),
               v_s[i]) for i in ch]
    a_rb = [jnp.where(incl[i], aa[i][c2:2 * c2, 0:c2], 0.0).astype(BF16) for i in ch]

    n1 = [jnp.where(xr < 8, a_ab[i], 0.0).astype(BF16) for i in ch]
    n2 = [_dot(n1[i], n1[i]).astype(BF16) for i in ch]
    t = [eye + n1[i].astype(F32) for i in ch]
    t = [t[i] + _dot(t[i].astype(BF16), n2[i]) for i in ch]
    n4 = [_dot(n2[i], n2[i]).astype(BF16) for i in ch]
    t = [t[i] + _dot(t[i].astype(BF16), n4[i]) for i in ch]
    m = 8
    while m < c:
        level = jnp.logical_and(xr >= m, xr < 2 * m)
        tb = [t[i].astype(BF16) for i in ch]
        x = [_dot(jnp.where(level, a_ab[i], 0.0).astype(BF16), tb[i]).astype(BF16) for i in ch]
        t = [t[i] + _dot(tb[i], x[i]) for i in ch]
        m *= 2

    u = [_dot(t[i].astype(BF16), (zs[i][0:c2] + av[i][0:c2]).astype(BF16)).astype(BF16) for i in ch]
    yu = [_dot(a_rb[i], u[i]) for i in ch]
    ds = [_dot_tn(jnp.concatenate([u[i], v_s[i]], axis=0), bk_end[i]) for i in ch]
    for i in ch:
        y_s = zs[i][c2:2 * c2] + av[i][c2:2 * c2] + yu[i]
        y_ref, cols = out[i]
        y_ref[:, cols] = y_s[0:c] + y_s[c:c2]
        s_ref[i] = s0[i] * dec_end[i] + ds[i]


def _wkv(r, v, nkk, kd, bd, lw, n_batch, seq, tc):
    n, aw = r.shape
    c = CHUNK
    nch = seq // c
    ncc = tc // c

    def fwd_blk(b, i):
        return b * nch + i

    def bwd_blk(b, i):
        return b * nch + jnp.where(i < ncc, ncc - 1 - i, nch - 1 - (i - ncc))

    def specs(blk, d):
        shared = pl.BlockSpec((c, aw), lambda b, i: (blk(b, i), 0))
        perdir = pl.BlockSpec((None, c, aw), lambda b, i: (d, blk(b, i), 0))
        return [shared, shared, shared, perdir, perdir, perdir], shared

    in_f, out_f = specs(fwd_blk, 0)
    in_b, out_b = specs(bwd_blk, 1)
    y = jax.ShapeDtypeStruct((n, aw), F32)
    return pl.pallas_call(
        _wkv_kernel,
        grid=(n_batch, nch),
        in_specs=in_f + in_b,
        out_specs=[out_f, out_b],
        out_shape=[y, y],
        scratch_shapes=[pltpu.VMEM((2 * (aw // PAIR), PAIR, PAIR), F32)],
        compiler_params=_cparams(("parallel", "arbitrary")),
        name="wkv_scan",
    )(r, v, nkk, kd, bd, lw, r, v, nkk, kd, bd, lw)


def _rwkv_out_kernel(yf_ref, yb_ref, r_ref, v_ref, kd_ref, zs_ref, rk_ref, lw_ref, lb_ref, gup_ref, e_ref, o_ref):
    e = e_ref[...]
    inv = 1.0 / A_HEAD
    y = yf_ref[...] + yb_ref[...]
    mu = _head_sum(y, e) * inv
    yc = y - mu
    var = _head_sum(yc * yc, e) * inv
    yh = yc * lax.rsqrt(var + LNX_EPS) * lw_ref[...] + lb_ref[...]
    bonus = _head_sum(r_ref[...] * (kd_ref[0] + kd_ref[1]) * rk_ref[...], e) * v_ref[...]
    zg = zs_ref[...][:, 2 * 2 * A_RANK:2 * 2 * A_RANK + A_GATE_RANK]
    g = _dot_hl(jax.nn.sigmoid(zg), gup_ref[...])
    o_ref[...] = ((yh + bonus) * g).astype(o_ref.dtype)


def _rwkv_output(yf, yb, r, v, kd, z_small, r_k, lnx_w, lnx_b, gate_up, e_head):
    n, aw = r.shape
    rt = RB
    rowspec = pl.BlockSpec((rt, aw), lambda i: (i, 0))
    dspec = pl.BlockSpec((2, rt, aw), lambda i: (0, i, 0))
    vec = pl.BlockSpec((1, aw), lambda i: (0, 0))
    return pl.pallas_call(
        _rwkv_out_kernel,
        grid=(n // rt,),
        in_specs=[
            rowspec, rowspec, rowspec, rowspec, dspec,
            pl.BlockSpec((rt, z_small.shape[1]), lambda i: (i, 0)),
            vec, vec, vec,
            pl.BlockSpec(gate_up.shape, lambda i: (0, 0)),
            pl.BlockSpec(e_head.shape, lambda i: (0, 0)),
        ],
        out_specs=rowspec,
        out_shape=jax.ShapeDtypeStruct((n, aw), BF16),
        compiler_params=_cparams(("parallel",)),
        name="rwkv_output",
    )(yf, yb, r, v, kd, z_small, r_k.reshape(1, aw), lnx_w.reshape(1, aw), lnx_b.reshape(1, aw),
      gate_up, e_head)


def _pool_kernel(u_ref, w_ref, ps_ref, o_ref, pad_ref, *, tc):
    seq = u_ref.shape[0]
    t_lat = seq - tc
    half_max = max(B_WINDOWS) // 2
    lat0 = half_max + tc + 2 * half_max
    gw = u_ref.shape[1]
    zeros = jnp.zeros((2 * half_max, gw), F32)
    pad_ref[0:half_max, :] = zeros[0:half_max]
    pad_ref[half_max:half_max + tc, :] = u_ref[0:tc, :]
    pad_ref[half_max + tc:lat0, :] = zeros
    pad_ref[lat0:lat0 + t_lat, :] = u_ref[tc:seq, :]
    pad_ref[lat0 + t_lat:lat0 + t_lat + half_max, :] = zeros[0:half_max]

    def segment(start, n, win):
        half = win // 2
        acc = pad_ref[start - half:start - half + n, :]
        for j in range(1 - half, half):
            acc = acc + pad_ref[start + j:start + j + n, :]
        t = lax.broadcasted_iota(jnp.int32, (n, gw), 0)
        cnt = jnp.minimum(t + half, n) - jnp.maximum(t - half, 0)
        return acc / cnt.astype(F32) - pad_ref[start:start + n, :]

    for gi, win in enumerate(B_WINDOWS):
        @pl.when(pl.program_id(1) == gi)
        def _(win=win):
            w = w_ref[0]
            ps = ps_ref[...]
            o_ref[0:tc, :] = (_dot(segment(half_max, tc, win).astype(BF16), w) * ps).astype(o_ref.dtype)
            o_ref[tc:seq, :] = (_dot(segment(lat0, t_lat, win).astype(BF16), w) * ps).astype(o_ref.dtype)


def _pool(z_pool, pool_w, pool_scale, n_batch, seq, tc):
    n, bw = z_pool.shape
    ng = len(B_WINDOWS)
    gw = bw // ng
    pad_rows = seq + 4 * (max(B_WINDOWS) // 2)
    return pl.pallas_call(
        functools.partial(_pool_kernel, tc=tc),
        grid=(n_batch, ng),
        in_specs=[
            pl.BlockSpec((seq, gw), lambda b, g: (b, g)),
            pl.BlockSpec((1, gw, gw), lambda b, g: (g, 0, 0)),
            pl.BlockSpec((1, gw), lambda b, g: (0, g)),
        ],
        out_specs=pl.BlockSpec((seq, gw), lambda b, g: (b, g)),
        out_shape=jax.ShapeDtypeStruct((n, bw), BF16),
        scratch_shapes=[pltpu.VMEM((pad_rows, gw), F32)],
        compiler_params=_cparams(("parallel", "parallel")),
        name="pool_mixer",
    )(z_pool, pool_w, pool_scale.reshape(1, bw))


def _attn_kernel(q_ref, k_ref, v_ref, lam_ref, g_ref, o_ref, *, tc, lam_init):
    lq = lam_ref[...]
    lam = (jnp.exp(jnp.sum(lq[0:1] * lq[1:2], axis=-1, keepdims=True))
           - jnp.exp(jnp.sum(lq[2:3] * lq[3:4], axis=-1, keepdims=True)) + lam_init)
    lane = lax.broadcasted_iota(jnp.int32, (1, C_VHEAD), 1)
    g = g_ref[...] * (1.0 - lam_init)

    def attend(nk):
        for hh in range(q_ref.shape[1] // C_VHEAD):
            hl = slice(hh * C_VHEAD, (hh + 1) * C_VHEAD)
            q = q_ref[:, hl]
            zero = jnp.zeros_like(q)
            k = k_ref[0:nk, hl]

            def probs(qh):
                s = _dot_nt(qh, k)
                e = jnp.exp(s - jnp.max(s, axis=-1, keepdims=True))
                return e, 1.0 / jnp.sum(e, axis=-1, keepdims=True)

            e0, i0 = probs(jnp.where(lane < C_HEAD, q, zero))
            e1, i1 = probs(jnp.where(lane < C_HEAD, zero, q))
            amap = (e0 * i0 - e1 * (lam * i1)).astype(BF16)
            o = _dot(amap, v_ref[0:nk, hl])
            o = o * lax.rsqrt(jnp.mean(o * o, axis=-1, keepdims=True) + 1e-5) * g
            o_ref[:, hl] = o.astype(o_ref.dtype)

    @pl.when(pl.program_id(2) == 0)
    def _():
        attend(tc)

    @pl.when(pl.program_id(2) != 0)
    def _():
        attend(k_ref.shape[0])


def _attention(q, k, v, lam_qk, subln_g, lam_init, n_batch, seq, tc):
    n, w = q.shape
    gw = _pick(w, (ATTN_HEADS_PER_STEP * C_VHEAD, C_VHEAD))
    nb = seq // RB
    kv = pl.BlockSpec((seq, gw), lambda b, h, i: (b, h))
    qo = pl.BlockSpec((RB, gw), lambda b, h, i: (b * nb + i, h))
    return pl.pallas_call(
        functools.partial(_attn_kernel, tc=tc, lam_init=lam_init),
        grid=(n_batch, w // gw, nb),
        in_specs=[
            qo, kv, kv,
            pl.BlockSpec(lam_qk.shape, lambda b, h, i: (0, 0)),
            pl.BlockSpec((1, C_VHEAD), lambda b, h, i: (0, 0)),
        ],
        out_specs=qo,
        out_shape=jax.ShapeDtypeStruct((n, w), BF16),
        compiler_params=_cparams(("parallel", "parallel", "parallel")),
        name="diff_attention",
    )(q, k, v, lam_qk, subln_g.reshape(1, C_VHEAD))


def _rope_tables(n_batch, t_lat, tc):
    n_freq = C_HEAD // 4
    inv = ROPE_BASE ** (-jnp.arange(n_freq, dtype=F32) / n_freq)
    rows = t_lat // GRID_W
    t_row = jnp.repeat(jnp.arange(rows, dtype=F32), GRID_W)
    t_col = jnp.tile(jnp.arange(GRID_W, dtype=F32), rows)
    ang_r = t_row[:, None] * inv
    ang_c = t_col[:, None] * inv
    cos = jnp.concatenate([jnp.cos(ang_r)] * 2 + [jnp.cos(ang_c)] * 2, axis=1)
    sin = jnp.concatenate([-jnp.sin(ang_r), jnp.sin(ang_r), -jnp.sin(ang_c), jnp.sin(ang_c)], axis=1)
    cos = jnp.concatenate([jnp.ones((tc, C_HEAD), F32), cos], axis=0)
    sin = jnp.concatenate([jnp.zeros((tc, C_HEAD), F32), sin], axis=0)
    cos = jnp.tile(jnp.tile(cos, (1, LANES // C_HEAD)), (n_batch, 1))
    sin = jnp.tile(jnp.tile(sin, (1, LANES // C_HEAD)), (n_batch, 1))
    return cos, sin


def kernel(x, c, ctx, c_ctx, w_ada, b_ada, norm_g, w_in, rkv_conv, decay_w0, decay_up, iclr_a0, iclr_up, gate_up, k_k, k_a, r_k, lnx_w, lnx_b, pool_w, pool_scale, lam_qk, subln_g, w_branch, w_out, w_ffn_in, w_ffn_out, final_g):
    n_batch, t_lat, d = x.shape
    tc = ctx.shape[1]
    depth = w_ada.shape[0]
    aw = k_k.shape[1]
    assert tc == RB and t_lat % RB == 0 and t_lat % GRID_W == 0 and aw % PAIR == 0
    seq = tc + t_lat
    nb = seq // RB
    n = n_batch * seq

    xa = jnp.concatenate([ctx, x], axis=1).reshape(n, d)
    mr = -(-(n_batch + 1) // SUBLANES) * SUBLANES
    c_all = jnp.zeros((mr, d), F32).at[:n_batch].set(c).at[n_batch].set(c_ctx)
    mods = _mod_tables(c_all, w_ada, b_ada)
    cos_t, sin_t = _rope_tables(n_batch, t_lat, tc)
    hid = lax.broadcasted_iota(jnp.int32, (MXU_COLS, MXU_COLS), 0) // A_HEAD
    e_head = (hid == hid.T).astype(BF16)

    o_small = 3 * aw
    o_pool = o_small + 2 * 2 * A_RANK + A_GATE_RANK
    o_q = o_pool + aw
    o_k = o_q + aw
    o_v = o_k + aw
    o_mix = o_v + aw

    h = _norm_mod(xa, norm_g[0, 0], mods[0], 0, 1, nb, n_batch)
    for l in range(depth):
        lam_init = 0.8 - 0.6 * math.exp(-0.3 * l)
        mod = mods[l]

        def wi(lo, hi):
            return w_in[l, :, lo:hi].astype(BF16)

        z_rkv = _mm(h, wi(0, o_small), F32)
        z_small = _mm(h, wi(o_small, o_pool), F32)
        z_pool = _mm(h, wi(o_pool, o_q), F32)
        q = _mm_rope(h, wi(o_q, o_k), cos_t, sin_t, C_HEAD ** -0.5, BF16)
        k = _mm_rope(h, wi(o_k, o_v), cos_t, sin_t, 1.0, BF16)
        v = _mm(h, wi(o_v, o_mix), BF16)
        z_mix = _mm(h, wi(o_mix, w_in.shape[2]), F32)

        r, va, nkk, kd, bd, lw = _rwkv_prepare(
            z_rkv, z_small, rkv_conv[l], decay_w0[l], decay_up[l], iclr_a0[l], iclr_up[l],
            k_k[l], k_a[l], e_head, seq, tc)
        yf, yb_dir = _wkv(r, va, nkk, kd, bd, lw, n_batch, seq, tc)
        ya = _rwkv_output(yf, yb_dir, r, va, kd, z_small, r_k[l], lnx_w[l], lnx_b[l], gate_up[l], e_head)
        yb = _pool(z_pool, pool_w[l].astype(BF16), pool_scale[l], n_batch, seq, tc)
        yc = _attention(q, k, v, lam_qk[l], subln_g[l], lam_init, n_batch, seq, tc)

        acc = _merge(ya, yb, yc, w_branch[l].astype(BF16), z_mix)
        xa, h2 = _mm_res_norm(acc, w_out[l].astype(BF16), xa, mod, 2, norm_g[l, 1], mod, 3, 4, nb, n_batch)
        ff = _mm_swiglu(h2, w_ffn_in[l].astype(BF16), BF16)
        if l + 1 < depth:
            xa, h = _mm_res_norm(ff, w_ffn_out[l].astype(BF16), xa, mod, 5,
                                 norm_g[l + 1, 0], mods[l + 1], 0, 1, nb, n_batch)
        else:
            xa = _mm_res(ff, w_ffn_out[l].astype(BF16), xa, mod, 5, nb, n_batch)

    out = _final_norm(xa, final_g, nb, n_batch)
    return out.reshape(n_batch, t_lat, d)
```

```python
import functools
import math

import jax
import jax.numpy as jnp
from jax import lax
from jax.experimental import pallas as pl
from jax.experimental.pallas import tpu as pltpu

F32 = jnp.float32
BF16 = jnp.bfloat16

LANES = 128
SUBLANES = 8
MXU_COLS = 256
VMEM_LIMIT = 56 * 1024 * 1024

A_HEAD = 64
A_RANK = 64
A_GATE_RANK = 128
LNX_EPS = 64e-5
B_WINDOWS = (2, 4, 8, 16)
C_HEAD = 64
C_VHEAD = 2 * C_HEAD
ROPE_BASE = 10000.0
GRID_W = 64
N_MOD = 6
N_BRANCH = 3
RB = 256
CHUNK = 64
PAIR = 2 * A_HEAD
HALO = SUBLANES
ATTN_HEADS_PER_STEP = 4


def _cparams(sem):
    return pltpu.CompilerParams(dimension_semantics=sem, vmem_limit_bytes=VMEM_LIMIT)


def _pick(n, cands):
    for c in cands:
        if n % c == 0:
            return c
    raise ValueError(f"no tile for {n} in {cands}")


def _mod_row(blk, nb, n_batch):
    return jnp.where(blk % nb == 0, n_batch, blk // nb)


def _split3(x):
    hi = x.astype(BF16)
    r1 = x - hi.astype(F32)
    mid = r1.astype(BF16)
    lo = (r1 - mid.astype(F32)).astype(BF16)
    return hi, mid, lo


def _dot(a, b):
    return jnp.dot(a, b, preferred_element_type=F32)


def _dot_nt(a, b):
    return lax.dot_general(a, b, (((1,), (1,)), ((), ())), preferred_element_type=F32)


def _dot_tn(a, b):
    return lax.dot_general(a, b, (((0,), (0,)), ((), ())), preferred_element_type=F32)


def _head_sum(x, e):
    hi = x.astype(BF16)
    lo = (x - hi.astype(F32)).astype(BF16)
    w = e.shape[0]
    return jnp.concatenate(
        [_dot(hi[:, s:s + w], e) + _dot(lo[:, s:s + w], e) for s in range(0, x.shape[1], w)], axis=1)


def _dot_e3(e, x):
    hi, mid, lo = _split3(x)
    return _dot(e, hi) + _dot(e, mid) + _dot(e, lo)


def _dot_hl(x, w):
    xh = x.astype(BF16)
    xl = (x - xh.astype(F32)).astype(BF16)
    wh = w.astype(BF16)
    wl = (w - wh.astype(F32)).astype(BF16)
    return _dot(xh, wh) + _dot(xh, wl) + _dot(xl, wh)


def _mod_kernel(c_ref, w_ref, b_ref, o_ref):
    c = c_ref[...]
    s = (c * jax.nn.sigmoid(c)).astype(BF16)
    o_ref[0] = _dot(s, w_ref[0].astype(BF16)) + b_ref[0]


def _mod_tables(c_all, w_ada, b_ada):
    nl, d, n6 = w_ada.shape
    mr = c_all.shape[0]
    tn = _pick(n6, (1024, 512, 256, 128))
    return pl.pallas_call(
        _mod_kernel,
        grid=(nl, n6 // tn),
        in_specs=[
            pl.BlockSpec((mr, d), lambda l, j: (0, 0)),
            pl.BlockSpec((1, d, tn), lambda l, j: (l, 0, j)),
            pl.BlockSpec((1, 1, tn), lambda l, j: (l, 0, j)),
        ],
        out_specs=pl.BlockSpec((1, mr, tn), lambda l, j: (l, 0, j)),
        out_shape=jax.ShapeDtypeStruct((nl, mr, n6), F32),
        compiler_params=_cparams(("parallel", "parallel")),
        name="adaln_table",
    )(c_all, w_ada, b_ada.reshape(nl, 1, n6))


def _norm_kernel(x_ref, g_ref, shift_ref, scale_ref, o_ref, *, nb, n_batch):
    row = _mod_row(pl.program_id(0), nb, n_batch)
    x = x_ref[...]
    y = x * lax.rsqrt(jnp.mean(x * x, axis=-1, keepdims=True) + 1e-6) * g_ref[...]
    y = y * (1.0 + scale_ref[pl.ds(row, 1), :]) + shift_ref[pl.ds(row, 1), :]
    o_ref[...] = y.astype(o_ref.dtype)


def _norm_mod(x, g, mod, shift_idx, scale_idx, nb, n_batch):
    n, d = x.shape
    mr = mod.shape[0]
    return pl.pallas_call(
        functools.partial(_norm_kernel, nb=nb, n_batch=n_batch),
        grid=(n // RB,),
        in_specs=[
            pl.BlockSpec((RB, d), lambda i: (i, 0)),
            pl.BlockSpec((1, d), lambda i: (0, 0)),
            pl.BlockSpec((mr, d), lambda i: (0, shift_idx)),
            pl.BlockSpec((mr, d), lambda i: (0, scale_idx)),
        ],
        out_specs=pl.BlockSpec((RB, d), lambda i: (i, 0)),
        out_shape=jax.ShapeDtypeStruct((n, d), BF16),
        compiler_params=_cparams(("parallel",)),
        name="norm_mod",
    )(x, g.reshape(1, d), mod, mod)


def _final_norm_kernel(x_ref, g_ref, o_ref):
    x = x_ref[...]
    o_ref[...] = x * lax.rsqrt(jnp.mean(x * x, axis=-1, keepdims=True) + 1e-6) * g_ref[...]


def _final_norm(x, g, nb, n_batch):
    n, d = x.shape
    nlb = nb - 1
    return pl.pallas_call(
        _final_norm_kernel,
        grid=(n_batch, nlb),
        in_specs=[
            pl.BlockSpec((RB, d), lambda b, i: (b * nb + i + 1, 0)),
            pl.BlockSpec((1, d), lambda b, i: (0, 0)),
        ],
        out_specs=pl.BlockSpec((RB, d), lambda b, i: (b * nlb + i, 0)),
        out_shape=jax.ShapeDtypeStruct((n_batch * nlb * RB, d), F32),
        compiler_params=_cparams(("parallel", "parallel")),
        name="final_norm",
    )(x, g.reshape(1, d))


def _mm_plain_kernel(a_ref, w_ref, o_ref):
    o_ref[...] = _dot(a_ref[...], w_ref[...]).astype(o_ref.dtype)


def _mm_tiles(n, k, m):
    tn = _pick(m, (1024, 512, 384, 256, 128)) if k <= 2048 else _pick(m, (512, 256, 128))
    tm = _pick(n, (1024, 512, 256))
    return tm, tn


def _mm(a, w, out_dtype):
    n, k = a.shape
    m = w.shape[1]
    tm, tn = _mm_tiles(n, k, m)
    return pl.pallas_call(
        _mm_plain_kernel,
        grid=(n // tm, m // tn),
        in_specs=[
            pl.BlockSpec((tm, k), lambda i, j: (i, 0)),
            pl.BlockSpec((k, tn), lambda i, j: (0, j)),
        ],
        out_specs=pl.BlockSpec((tm, tn), lambda i, j: (i, j)),
        out_shape=jax.ShapeDtypeStruct((n, m), out_dtype),
        compiler_params=_cparams(("parallel", "parallel")),
        name="mm_plain",
    )(a, w)


def _mm_rope_kernel(a_ref, w_ref, cos_ref, sin_ref, o_ref, *, scale):
    acc = _dot(a_ref[...], w_ref[...])
    lane = lax.broadcasted_iota(jnp.int32, (1, LANES), 1)
    first_half = (lane % (C_HEAD // 2)) < (C_HEAD // 4)
    cos = cos_ref[...]
    sin = sin_ref[...]
    for s in range(acc.shape[1] // LANES):
        u = acc[:, s * LANES:(s + 1) * LANES]
        partner = jnp.where(first_half,
                            pltpu.roll(u, LANES - C_HEAD // 4, 1),
                            pltpu.roll(u, C_HEAD // 4, 1))
        o_ref[:, s * LANES:(s + 1) * LANES] = ((u * cos + partner * sin) * scale).astype(o_ref.dtype)


def _mm_rope(a, w, cos_t, sin_t, scale, out_dtype):
    n, k = a.shape
    m = w.shape[1]
    tm, tn = _mm_tiles(n, k, m)
    return pl.pallas_call(
        functools.partial(_mm_rope_kernel, scale=scale),
        grid=(n // tm, m // tn),
        in_specs=[
            pl.BlockSpec((tm, k), lambda i, j: (i, 0)),
            pl.BlockSpec((k, tn), lambda i, j: (0, j)),
            pl.BlockSpec((tm, LANES), lambda i, j: (i, 0)),
            pl.BlockSpec((tm, LANES), lambda i, j: (i, 0)),
        ],
        out_specs=pl.BlockSpec((tm, tn), lambda i, j: (i, j)),
        out_shape=jax.ShapeDtypeStruct((n, m), out_dtype),
        compiler_params=_cparams(("parallel", "parallel")),
        name="mm_rope",
    )(a, w, cos_t, sin_t)


def _mm_swiglu_kernel(a_ref, wg_ref, wu_ref, o_ref):
    a = a_ref[...]
    g = _dot(a, wg_ref[...])
    u = _dot(a, wu_ref[...])
    o_ref[...] = (g * jax.nn.sigmoid(g) * u).astype(o_ref.dtype)


def _mm_swiglu(a, w, out_dtype):
    n, k = a.shape
    f = w.shape[1] // 2
    tm, tn = _mm_tiles(n, k, f)
    nj = f // tn
    return pl.pallas_call(
        _mm_swiglu_kernel,
        grid=(n // tm, nj),
        in_specs=[
            pl.BlockSpec((tm, k), lambda i, j: (i, 0)),
            pl.BlockSpec((k, tn), lambda i, j: (0, j)),
            pl.BlockSpec((k, tn), lambda i, j: (0, j + nj)),
        ],
        out_specs=pl.BlockSpec((tm, tn), lambda i, j: (i, j)),
        out_shape=jax.ShapeDtypeStruct((n, f), out_dtype),
        compiler_params=_cparams(("parallel", "parallel")),
        name="mm_swiglu",
    )(a, w, w)


def _mm_res_kernel(a_ref, w_ref, res_ref, gate_ref, o_ref, *, nb, n_batch):
    acc = _dot(a_ref[...], w_ref[...])
    tm = acc.shape[0]
    for s in range(tm // RB):
        row = _mod_row(pl.program_id(0) * (tm // RB) + s, nb, n_batch)
        gate = gate_ref[pl.ds(row, 1), :]
        rows = slice(s * RB, (s + 1) * RB)
        o_ref[rows, :] = res_ref[rows, :] + gate * acc[rows, :]


def _mm_res(a, w, res, mod, gate_idx, nb, n_batch):
    n, k = a.shape
    m = w.shape[1]
    mr = mod.shape[0]
    tm, tn = _mm_tiles(n, k, m)
    goff = gate_idx * (m // tn)
    return pl.pallas_call(
        functools.partial(_mm_res_kernel, nb=nb, n_batch=n_batch),
        grid=(n // tm, m // tn),
        in_specs=[
            pl.BlockSpec((tm, k), lambda i, j: (i, 0)),
            pl.BlockSpec((k, tn), lambda i, j: (0, j)),
            pl.BlockSpec((tm, tn), lambda i, j: (i, j)),
            pl.BlockSpec((mr, tn), lambda i, j: (0, goff + j)),
        ],
        out_specs=pl.BlockSpec((tm, tn), lambda i, j: (i, j)),
        out_shape=jax.ShapeDtypeStruct((n, m), F32),
        compiler_params=_cparams(("parallel", "parallel")),
        name="mm_residual",
    )(a, w, res, mod)


def _mm_res_norm_kernel(a_ref, w_ref, res_ref, gate_ref, g_ref, shift_ref, scale_ref, x_ref, h_ref, *, nb, n_batch):
    acc = _dot(a_ref[...], w_ref[...])
    tm = acc.shape[0]
    for s in range(tm // RB):
        row = _mod_row(pl.program_id(0) * (tm // RB) + s, nb, n_batch)
        rows = slice(s * RB, (s + 1) * RB)
        x = res_ref[rows, :] + gate_ref[pl.ds(row, 1), :] * acc[rows, :]
        x_ref[rows, :] = x
        y = x * lax.rsqrt(jnp.mean(x * x, axis=-1, keepdims=True) + 1e-6) * g_ref[...]
        h_ref[rows, :] = (y * (1.0 + scale_ref[pl.ds(row, 1), :]) + shift_ref[pl.ds(row, 1), :]).astype(h_ref.dtype)


def _mm_res_norm(a, w, res, mod, gate_idx, g_next, mod_next, shift_idx, scale_idx, nb, n_batch):
    n, k = a.shape
    d = w.shape[1]
    mr = mod.shape[0]
    tm = _pick(n, (512, 256)) if k <= 2048 else RB
    row = lambda i: (i, 0)
    return pl.pallas_call(
        functools.partial(_mm_res_norm_kernel, nb=nb, n_batch=n_batch),
        grid=(n // tm,),
        in_specs=[
            pl.BlockSpec((tm, k), row),
            pl.BlockSpec((k, d), lambda i: (0, 0), pipeline_mode=pl.Buffered(1)),
            pl.BlockSpec((tm, d), row),
            pl.BlockSpec((mr, d), lambda i: (0, gate_idx)),
            pl.BlockSpec((1, d), lambda i: (0, 0)),
            pl.BlockSpec((mr, d), lambda i: (0, shift_idx)),
            pl.BlockSpec((mr, d), lambda i: (0, scale_idx)),
        ],
        out_specs=[pl.BlockSpec((tm, d), row), pl.BlockSpec((tm, d), row)],
        out_shape=[jax.ShapeDtypeStruct((n, d), F32), jax.ShapeDtypeStruct((n, d), BF16)],
        compiler_params=_cparams(("parallel",)),
        name="mm_residual_norm",
    )(a, w, res, mod, g_next.reshape(1, d), mod_next, mod_next)


def _merge_kernel(ya_ref, yb_ref, yc_ref, w_ref, za_ref, zb_ref, zc_ref, o_ref):
    acc = jax.nn.sigmoid(za_ref[...]) * _dot(ya_ref[...], w_ref[0])
    acc += jax.nn.sigmoid(zb_ref[...]) * _dot(yb_ref[...], w_ref[1])
    acc += jax.nn.sigmoid(zc_ref[...]) * _dot(yc_ref[...], w_ref[2])
    o_ref[...] = acc.astype(o_ref.dtype)


def _merge(ya, yb, yc, w_branch, z_mix):
    n, kw = ya.shape
    d = w_branch.shape[2]
    tm = _pick(n, (1024, 512, 256))
    tn = _pick(d, (512, 256, 128))
    nj = d // tn
    yspec = pl.BlockSpec((tm, kw), lambda i, j: (i, 0))
    return pl.pallas_call(
        _merge_kernel,
        grid=(n // tm, nj),
        in_specs=[
            yspec, yspec, yspec,
            pl.BlockSpec((N_BRANCH, kw, tn), lambda i, j: (0, 0, j)),
            pl.BlockSpec((tm, tn), lambda i, j: (i, j)),
            pl.BlockSpec((tm, tn), lambda i, j: (i, nj + j)),
            pl.BlockSpec((tm, tn), lambda i, j: (i, 2 * nj + j)),
        ],
        out_specs=pl.BlockSpec((tm, tn), lambda i, j: (i, j)),
        out_shape=jax.ShapeDtypeStruct((n, d), BF16),
        compiler_params=_cparams(("parallel", "parallel")),
        name="merge",
    )(ya, yb, yc, w_branch, z_mix, z_mix, z_mix)


def _prep_kernel(zr_ref, zp_ref, zn_ref, zs_ref, conv_ref, w0_ref, wup_ref, a0_ref, aup_ref,
                 kk_ref, ka_ref, e_ref,
                 r_ref, v_ref, nkk_ref, kd_ref, bd_ref, lw_ref, up_ref, dn_ref, *, seq, tc):
    rt = zr_ref.shape[0]
    aw = r_ref.shape[1]
    pos = (pl.program_id(0) * rt) % seq
    prev_ok = jnp.logical_and(pos != 0, pos != tc)
    end = pos + rt
    next_ok = jnp.logical_and(end != seq, end != tc)
    u = zr_ref[...]
    up_ref[...] = pltpu.roll(u, 1, 0)
    up_ref[0:1, :] = jnp.where(prev_ok, zp_ref[HALO - 1:HALO, :], 0.0)
    dn_ref[...] = pltpu.roll(u, rt - 1, 0)
    dn_ref[rt - 1:rt, :] = jnp.where(next_ok, zn_ref[0:1, :], 0.0)
    c = up_ref[...] * conv_ref[0:1, :] + u * conv_ref[1:2, :] + dn_ref[...] * conv_ref[2:3, :]
    r = c[:, 0:aw]
    k = c[:, aw:2 * aw]
    v = c[:, 2 * aw:3 * aw]
    r_ref[...] = r
    v_ref[...] = v
    kx = k * kk_ref[...]
    kk = kx * lax.rsqrt(jnp.maximum(_head_sum(kx * kx, e_ref[...]), 1e-24))
    nkk_ref[...] = -kk
    zs = zs_ref[...]
    ka = ka_ref[...]
    for d in range(2):
        zd = zs[:, d * A_RANK:(d + 1) * A_RANK]
        w_raw = w0_ref[d:d + 1, :] + _dot_hl(jnp.tanh(zd), wup_ref[d])
        lw_ref[d] = -math.exp(-0.5) * jax.nn.sigmoid(w_raw)
        za = zs[:, 2 * A_RANK + d * A_RANK:2 * A_RANK + (d + 1) * A_RANK]
        a = jax.nn.sigmoid(a0_ref[d:d + 1, :] + _dot_hl(za, aup_ref[d]))
        kd_ref[d] = k * (1.0 + (a - 1.0) * ka)
        bd_ref[d] = kk * a


def _rwkv_prepare(z_rkv, z_small, conv_w, w0, w_up, a0, a_up, k_k, k_a, e_head, seq, tc):
    n, w3 = z_rkv.shape
    aw = w3 // 3
    rt = RB
    nh = rt // HALO
    last = n // HALO - 1
    full2 = lambda i: (0, 0)
    full3 = lambda i: (0, 0, 0)
    row = jax.ShapeDtypeStruct((n, aw), F32)
    drow = jax.ShapeDtypeStruct((2, n, aw), F32)
    return pl.pallas_call(
        functools.partial(_prep_kernel, seq=seq, tc=tc),
        grid=(n // rt,),
        in_specs=[
            pl.BlockSpec((rt, w3), lambda i: (i, 0)),
            pl.BlockSpec((HALO, w3), lambda i: (jnp.maximum(i * nh - 1, 0), 0)),
            pl.BlockSpec((HALO, w3), lambda i: (jnp.minimum((i + 1) * nh, last), 0)),
            pl.BlockSpec((rt, z_small.shape[1]), lambda i: (i, 0)),
            pl.BlockSpec(conv_w.shape, full2),
            pl.BlockSpec(w0.shape, full2),
            pl.BlockSpec(w_up.shape, full3),
            pl.BlockSpec(a0.shape, full2),
            pl.BlockSpec(a_up.shape, full3),
            pl.BlockSpec((1, aw), full2),
            pl.BlockSpec((1, aw), full2),
            pl.BlockSpec(e_head.shape, full2),
        ],
        out_specs=[
            pl.BlockSpec((rt, aw), lambda i: (i, 0)),
            pl.BlockSpec((rt, aw), lambda i: (i, 0)),
            pl.BlockSpec((rt, aw), lambda i: (i, 0)),
            pl.BlockSpec((2, rt, aw), lambda i: (0, i, 0)),
            pl.BlockSpec((2, rt, aw), lambda i: (0, i, 0)),
            pl.BlockSpec((2, rt, aw), lambda i: (0, i, 0)),
        ],
        out_shape=[row, row, row, drow, drow, drow],
        scratch_shapes=[pltpu.VMEM((rt, w3), F32), pltpu.VMEM((rt, w3), F32)],
        compiler_params=_cparams(("parallel",)),
        name="rwkv_prepare",
    )(z_rkv, z_rkv, z_rkv, z_small, conv_w, w0, w_up, a0, a_up,
      k_k.reshape(1, aw), k_a.reshape(1, aw), e_head)


def _wkv_kernel(rf_ref, vf_ref, af_ref, kf_ref, bf_ref, lwf_ref, rb_ref, vb_ref, ab_ref, kb_ref, bb_ref, lwb_ref,
                yf_ref, yb_ref, s_ref):
    c = rf_ref.shape[0]
    c2 = 2 * c
    n_pair = rf_ref.shape[1] // PAIR

    @pl.when(pl.program_id(1) == 0)
    def _():
        s_ref[...] = jnp.zeros_like(s_ref)

    ti = lax.broadcasted_iota(jnp.int32, (c, c), 0)
    si = lax.broadcasted_iota(jnp.int32, (c, c), 1)
    i2 = lax.broadcasted_iota(jnp.int32, (c2, c2), 0)
    j2 = lax.broadcasted_iota(jnp.int32, (c2, c2), 1)
    col_minus_row = j2 % c - i2 % c
    xr = i2 ^ j2
    eye = (i2 == j2).astype(F32)
    lane = lax.broadcasted_iota(jnp.int32, (1, PAIR), 1)
    m0 = (lane < A_HEAD).astype(F32)
    m1 = 1.0 - m0

    def stack(x):
        return jnp.concatenate([x * m0, x * m1], axis=0)

    zl, zr, v_s, bk_end, dec_end, strict, incl, out = [], [], [], [], [], [], [], []
    for r_ref, v_ref, a_ref, k_ref, b_ref, lw_ref, y_ref, sgn in (
            (rf_ref, vf_ref, af_ref, kf_ref, bf_ref, lwf_ref, yf_ref, 1),
            (rb_ref, vb_ref, ab_ref, kb_ref, bb_ref, lwb_ref, yb_ref, -1)):
        cum = jnp.where((si - ti) * sgn <= 0, 1.0, 0.0).astype(BF16)
        cs_all = _dot_e3(cum, lw_ref[...])
        last = c - 1 if sgn > 0 else 0
        for p in range(n_pair):
            cols = slice(p * PAIR, (p + 1) * PAIR)
            lw = lw_ref[:, cols]
            cs = cs_all[:, cols]
            cs_end = cs[last:last + 1, :]
            g_inv = jnp.exp(-cs)
            g_end = jnp.exp(cs_end - cs)
            kd = k_ref[:, cols]
            bd = b_ref[:, cols]
            a_t = stack(a_ref[:, cols] * jnp.exp(cs - lw))
            r_t = stack(r_ref[:, cols] * jnp.exp(cs))
            zl.append(jnp.concatenate([a_t, r_t], axis=0).astype(BF16))
            zr.append(jnp.concatenate([stack(bd * g_inv), stack(kd * g_inv)], axis=0).astype(BF16))
            v_s.append(stack(v_ref[:, cols]).astype(BF16))
            bk_end.append(jnp.concatenate([stack(bd * g_end), stack(kd * g_end)], axis=0).astype(BF16))
            dec_end.append(jnp.exp(cs_end))
            strict.append(col_minus_row * sgn < 0)
            incl.append(col_minus_row * sgn <= 0)
            out.append((y_ref, cols))

    ch = range(len(zl))
    aa = [_dot_nt(zl[i], zr[i]) for i in ch]
    s0 = [s_ref[i] for i in ch]
    zs = [_dot_nt(zl[i], s0[i].astype(BF16)) for i in ch]
    a_ab = [jnp.where(strict[i], aa[i][0:c2, 0:c2], 0.0) for i in ch]
    av = [_dot(jnp.concatenate([jnp.where(strict[i], aa[i][0:c2, c2:2 * c2], 0.0),
                                jnp.where(incl[i], aa[i][c2:2 * c2, c2:2 * c2], 0.0)], axis=0).astype(BF16),
               v_s[i]) for i in ch]
    a_rb = [jnp.where(incl[i], aa[i][c2:2 * c2, 0:c2], 0.0).astype(BF16) for i in ch]

    n1 = [jnp.where(xr < 8, a_ab[i], 0.0).astype(BF16) for i in ch]
    n2 = [_dot(n1[i], n1[i]).astype(BF16) for i in ch]
    t = [eye + n1[i].astype(F32) for i in ch]
    t = [t[i] + _dot(t[i].astype(BF16), n2[i]) for i in ch]
    n4 = [_dot(n2[i], n2[i]).astype(BF16) for i in ch]
    t = [t[i] + _dot(t[i].astype(BF16), n4[i]) for i in ch]
    m = 8
    while m < c:
        level = jnp.logical_and(xr >= m, xr < 2 * m)
        tb = [t[i].astype(BF16) for i in ch]
        x = [_dot(jnp.where(level, a_ab[i], 0.0).astype(BF16), tb[i]).astype(BF16) for i in ch]
        t = [t[i] + _dot(tb[i], x[i]) for i in ch]
        m *= 2

    u = [_dot(t[i].astype(BF16), (zs[i][0:c2] + av[i][0:c2]).astype(BF16)).astype(BF16) for i in ch]
    yu = [_dot(a_rb[i], u[i]) for i in ch]
    ds = [_dot_tn(jnp.concatenate([u[i], v_s[i]], axis=0), bk_end[i]) for i in ch]
    for i in ch:
        y_s = zs[i][c2:2 * c2] + av[i][c2:2 * c2] + yu[i]
        y_ref, cols = out[i]
        y_ref[:, cols] = y_s[0:c] + y_s[c:c2]
        s_ref[i] = s0[i] * dec_end[i] + ds[i]


def _wkv(r, v, nkk, kd, bd, lw, n_batch, seq, tc):
    n, aw = r.shape
    c = CHUNK
    nch = seq // c
    ncc = tc // c

    def fwd_blk(b, i):
        return b * nch + i

    def bwd_blk(b, i):
        return b * nch + jnp.where(i < ncc, ncc - 1 - i, nch - 1 - (i - ncc))

    def specs(blk, d):
        shared = pl.BlockSpec((c, aw), lambda b, i: (blk(b, i), 0))
        perdir = pl.BlockSpec((None, c, aw), lambda b, i: (d, blk(b, i), 0))
        return [shared, shared, shared, perdir, perdir, perdir], shared

    in_f, out_f = specs(fwd_blk, 0)
    in_b, out_b = specs(bwd_blk, 1)
    y = jax.ShapeDtypeStruct((n, aw), F32)
    return pl.pallas_call(
        _wkv_kernel,
        grid=(n_batch, nch),
        in_specs=in_f + in_b,
        out_specs=[out_f, out_b],
        out_shape=[y, y],
        scratch_shapes=[pltpu.VMEM((2 * (aw // PAIR), PAIR, PAIR), F32)],
        compiler_params=_cparams(("parallel", "arbitrary")),
        name="wkv_scan",
    )(r, v, nkk, kd, bd, lw, r, v, nkk, kd, bd, lw)


def _rwkv_out_kernel(yf_ref, yb_ref, r_ref, v_ref, kd_ref, zs_ref, rk_ref, lw_ref, lb_ref, gup_ref, e_ref, o_ref):
    e = e_ref[...]
    inv = 1.0 / A_HEAD
    y = yf_ref[...] + yb_ref[...]
    mu = _head_sum(y, e) * inv
    yc = y - mu
    var = _head_sum(yc * yc, e) * inv
    yh = yc * lax.rsqrt(var + LNX_EPS) * lw_ref[...] + lb_ref[...]
    bonus = _head_sum(r_ref[...] * (kd_ref[0] + kd_ref[1]) * rk_ref[...], e) * v_ref[...]
    zg = zs_ref[...][:, 2 * 2 * A_RANK:2 * 2 * A_RANK + A_GATE_RANK]
    g = _dot_hl(jax.nn.sigmoid(zg), gup_ref[...])
    o_ref[...] = ((yh + bonus) * g).astype(o_ref.dtype)


def _rwkv_output(yf, yb, r, v, kd, z_small, r_k, lnx_w, lnx_b, gate_up, e_head):
    n, aw = r.shape
    rt = RB
    rowspec = pl.BlockSpec((rt, aw), lambda i: (i, 0))
    dspec = pl.BlockSpec((2, rt, aw), lambda i: (0, i, 0))
    vec = pl.BlockSpec((1, aw), lambda i: (0, 0))
    return pl.pallas_call(
        _rwkv_out_kernel,
        grid=(n // rt,),
        in_specs=[
            rowspec, rowspec, rowspec, rowspec, dspec,
            pl.BlockSpec((rt, z_small.shape[1]), lambda i: (i, 0)),
            vec, vec, vec,
            pl.BlockSpec(gate_up.shape, lambda i: (0, 0)),
            pl.BlockSpec(e_head.shape, lambda i: (0, 0)),
        ],
        out_specs=rowspec,
        out_shape=jax.ShapeDtypeStruct((n, aw), BF16),
        compiler_params=_cparams(("parallel",)),
        name="rwkv_output",
    )(yf, yb, r, v, kd, z_small, r_k.reshape(1, aw), lnx_w.reshape(1, aw), lnx_b.reshape(1, aw),
      gate_up, e_head)


def _pool_kernel(u_ref, w_ref, ps_ref, o_ref, pad_ref, *, tc):
    seq = u_ref.shape[0]
    t_lat = seq - tc
    half_max = max(B_WINDOWS) // 2
    lat0 = half_max + tc + 2 * half_max
    gw = u_ref.shape[1]
    zeros = jnp.zeros((2 * half_max, gw), F32)
    pad_ref[0:half_max, :] = zeros[0:half_max]
    pad_ref[half_max:half_max + tc, :] = u_ref[0:tc, :]
    pad_ref[half_max + tc:lat0, :] = zeros
    pad_ref[lat0:lat0 + t_lat, :] = u_ref[tc:seq, :]
    pad_ref[lat0 + t_lat:lat0 + t_lat + half_max, :] = zeros[0:half_max]

    def segment(start, n, win):
        half = win // 2
        acc = pad_ref[start - half:start - half + n, :]
        for j in range(1 - half, half):
            acc = acc + pad_ref[start + j:start + j + n, :]
        t = lax.broadcasted_iota(jnp.int32, (n, gw), 0)
        cnt = jnp.minimum(t + half, n) - jnp.maximum(t - half, 0)
        return acc / cnt.astype(F32) - pad_ref[start:start + n, :]

    for gi, win in enumerate(B_WINDOWS):
        @pl.when(pl.program_id(1) == gi)
        def _(win=win):
            w = w_ref[0]
            ps = ps_ref[...]
            o_ref[0:tc, :] = (_dot(segment(half_max, tc, win).astype(BF16), w) * ps).astype(o_ref.dtype)
            o_ref[tc:seq, :] = (_dot(segment(lat0, t_lat, win).astype(BF16), w) * ps).astype(o_ref.dtype)


def _pool(z_pool, pool_w, pool_scale, n_batch, seq, tc):
    n, bw = z_pool.shape
    ng = len(B_WINDOWS)
    gw = bw // ng
    pad_rows = seq + 4 * (max(B_WINDOWS) // 2)
    return pl.pallas_call(
        functools.partial(_pool_kernel, tc=tc),
        grid=(n_batch, ng),
        in_specs=[
            pl.BlockSpec((seq, gw), lambda b, g: (b, g)),
            pl.BlockSpec((1, gw, gw), lambda b, g: (g, 0, 0)),
            pl.BlockSpec((1, gw), lambda b, g: (0, g)),
        ],
        out_specs=pl.BlockSpec((seq, gw), lambda b, g: (b, g)),
        out_shape=jax.ShapeDtypeStruct((n, bw), BF16),
        scratch_shapes=[pltpu.VMEM((pad_rows, gw), F32)],
        compiler_params=_cparams(("parallel", "parallel")),
        name="pool_mixer",
    )(z_pool, pool_w, pool_scale.reshape(1, bw))


def _attn_kernel(q_ref, k_ref, v_ref, lam_ref, g_ref, o_ref, *, tc, lam_init):
    lq = lam_ref[...]
    lam = (jnp.exp(jnp.sum(lq[0:1] * lq[1:2], axis=-1, keepdims=True))
           - jnp.exp(jnp.sum(lq[2:3] * lq[3:4], axis=-1, keepdims=True)) + lam_init)
    lane = lax.broadcasted_iota(jnp.int32, (1, C_VHEAD), 1)
    g = g_ref[...] * (1.0 - lam_init)

    def attend(nk):
        for hh in range(q_ref.shape[1] // C_VHEAD):
            hl = slice(hh * C_VHEAD, (hh + 1) * C_VHEAD)
            q = q_ref[:, hl]
            zero = jnp.zeros_like(q)
            k = k_ref[0:nk, hl]

            def probs(qh):
                s = _dot_nt(qh, k)
                e = jnp.exp(s - jnp.max(s, axis=-1, keepdims=True))
                return e, 1.0 / jnp.sum(e, axis=-1, keepdims=True)

            e0, i0 = probs(jnp.where(lane < C_HEAD, q, zero))
            e1, i1 = probs(jnp.where(lane < C_HEAD, zero, q))
            amap = (e0 * i0 - e1 * (lam * i1)).astype(BF16)
            o = _dot(amap, v_ref[0:nk, hl])
            o = o * lax.rsqrt(jnp.mean(o * o, axis=-1, keepdims=True) + 1e-5) * g
            o_ref[:, hl] = o.astype(o_ref.dtype)

    @pl.when(pl.program_id(2) == 0)
    def _():
        attend(tc)

    @pl.when(pl.program_id(2) != 0)
    def _():
        attend(k_ref.shape[0])


def _attention(q, k, v, lam_qk, subln_g, lam_init, n_batch, seq, tc):
    n, w = q.shape
    gw = _pick(w, (ATTN_HEADS_PER_STEP * C_VHEAD, C_VHEAD))
    nb = seq // RB
    kv = pl.BlockSpec((seq, gw), lambda b, h, i: (b, h))
    qo = pl.BlockSpec((RB, gw), lambda b, h, i: (b * nb + i, h))
    return pl.pallas_call(
        functools.partial(_attn_kernel, tc=tc, lam_init=lam_init),
        grid=(n_batch, w // gw, nb),
        in_specs=[
            qo, kv, kv,
            pl.BlockSpec(lam_qk.shape, lambda b, h, i: (0, 0)),
            pl.BlockSpec((1, C_VHEAD), lambda b, h, i: (0, 0)),
        ],
        out_specs=qo,
        out_shape=jax.ShapeDtypeStruct((n, w), BF16),
        compiler_params=_cparams(("parallel", "parallel", "parallel")),
        name="diff_attention",
    )(q, k, v, lam_qk, subln_g.reshape(1, C_VHEAD))


def _rope_tables(n_batch, t_lat, tc):
    n_freq = C_HEAD // 4
    inv = ROPE_BASE ** (-jnp.arange(n_freq, dtype=F32) / n_freq)
    rows = t_lat // GRID_W
    t_row = jnp.repeat(jnp.arange(rows, dtype=F32), GRID_W)
    t_col = jnp.tile(jnp.arange(GRID_W, dtype=F32), rows)
    ang_r = t_row[:, None] * inv
    ang_c = t_col[:, None] * inv
    cos = jnp.concatenate([jnp.cos(ang_r)] * 2 + [jnp.cos(ang_c)] * 2, axis=1)
    sin = jnp.concatenate([-jnp.sin(ang_r), jnp.sin(ang_r), -jnp.sin(ang_c), jnp.sin(ang_c)], axis=1)
    cos = jnp.concatenate([jnp.ones((tc, C_HEAD), F32), cos], axis=0)
    sin = jnp.concatenate([jnp.zeros((tc, C_HEAD), F32), sin], axis=0)
    cos = jnp.tile(jnp.tile(cos, (1, LANES // C_HEAD)), (n_batch, 1))
    sin = jnp.tile(jnp.tile(sin, (1, LANES // C_HEAD)), (n_batch, 1))
    return cos, sin


def kernel(x, c, ctx, c_ctx, w_ada, b_ada, norm_g, w_in, rkv_conv, decay_w0, decay_up, iclr_a0, iclr_up, gate_up, k_k, k_a, r_k, lnx_w, lnx_b, pool_w, pool_scale, lam_qk, subln_g, w_branch, w_out, w_ffn_in, w_ffn_out, final_g):
    n_batch, t_lat, d = x.shape
    tc = ctx.shape[1]
    depth = w_ada.shape[0]
    aw = k_k.shape[1]
    assert tc == RB and t_lat % RB == 0 and t_lat % GRID_W == 0 and aw % PAIR == 0
    seq = tc + t_lat
    nb = seq // RB
    n = n_batch * seq

    xa = jnp.concatenate([ctx, x], axis=1).reshape(n, d)
    mr = -(-(n_batch + 1) // SUBLANES) * SUBLANES
    c_all = jnp.zeros((mr, d), F32).at[:n_batch].set(c).at[n_batch].set(c_ctx)
    mods = _mod_tables(c_all, w_ada, b_ada)
    cos_t, sin_t = _rope_tables(n_batch, t_lat, tc)
    hid = lax.broadcasted_iota(jnp.int32, (MXU_COLS, MXU_COLS), 0) // A_HEAD
    e_head = (hid == hid.T).astype(BF16)

    o_small = 3 * aw
    o_pool = o_small + 2 * 2 * A_RANK + A_GATE_RANK
    o_q = o_pool + aw
    o_k = o_q + aw
    o_v = o_k + aw
    o_mix = o_v + aw

    h = _norm_mod(xa, norm_g[0, 0], mods[0], 0, 1, nb, n_batch)
    for l in range(depth):
        lam_init = 0.8 - 0.6 * math.exp(-0.3 * l)
        mod = mods[l]

        def wi(lo, hi):
            return w_in[l, :, lo:hi].astype(BF16)

        z_rkv = _mm(h, wi(0, o_small), F32)
        z_small = _mm(h, wi(o_small, o_pool), F32)
        z_pool = _mm(h, wi(o_pool, o_q), F32)
        q = _mm_rope(h, wi(o_q, o_k), cos_t, sin_t, C_HEAD ** -0.5, BF16)
        k = _mm_rope(h, wi(o_k, o_v), cos_t, sin_t, 1.0, BF16)
        v = _mm(h, wi(o_v, o_mix), BF16)
        z_mix = _mm(h, wi(o_mix, w_in.shape[2]), F32)

        r, va, nkk, kd, bd, lw = _rwkv_prepare(
            z_rkv, z_small, rkv_conv[l], decay_w0[l], decay_up[l], iclr_a0[l], iclr_up[l],
            k_k[l], k_a[l], e_head, seq, tc)
        yf, yb_dir = _wkv(r, va, nkk, kd, bd, lw, n_batch, seq, tc)
        ya = _rwkv_output(yf, yb_dir, r, va, kd, z_small, r_k[l], lnx_w[l], lnx_b[l], gate_up[l], e_head)
        yb = _pool(z_pool, pool_w[l].astype(BF16), pool_scale[l], n_batch, seq, tc)
        yc = _attention(q, k, v, lam_qk[l], subln_g[l], lam_init, n_batch, seq, tc)

        acc = _merge(ya, yb, yc, w_branch[l].astype(BF16), z_mix)
        xa, h2 = _mm_res_norm(acc, w_out[l].astype(BF16), xa, mod, 2, norm_g[l, 1], mod, 3, 4, nb, n_batch)
        ff = _mm_swiglu(h2, w_ffn_in[l].astype(BF16), BF16)
        if l + 1 < depth:
            xa, h = _mm_res_norm(ff, w_ffn_out[l].astype(BF16), xa, mod, 5,
                                 norm_g[l + 1, 0], mods[l + 1], 0, 1, nb, n_batch)
        else:
            xa = _mm_res(ff, w_ffn_out[l].astype(BF16), xa, mod, 5, nb, n_batch)

    out = _final_norm(xa, final_g, nb, n_batch)
    return out.reshape(n_batch, t_lat, d)
```

```python
import functools
import math

import jax
import jax.numpy as jnp
from jax import lax
from jax.experimental import pallas as pl
from jax.experimental.pallas import tpu as pltpu

F32 = jnp.float32
BF16 = jnp.bfloat16

LANES = 128
SUBLANES = 8
MXU_COLS = 256
VMEM_LIMIT = 56 * 1024 * 1024

A_HEAD = 64
A_RANK = 64
A_GATE_RANK = 128
LNX_EPS = 64e-5
B_WINDOWS = (2, 4, 8, 16)
C_HEAD = 64
C_VHEAD = 2 * C_HEAD
ROPE_BASE = 10000.0
GRID_W = 64
N_MOD = 6
N_BRANCH = 3
RB = 256
CHUNK = 64
PAIR = 2 * A_HEAD
HALO = SUBLANES
ATTN_HEADS_PER_STEP = 4
KEY_TILE = MXU_COLS


def _cparams(sem):
    return pltpu.CompilerParams(dimension_semantics=sem, vmem_limit_bytes=VMEM_LIMIT)


def _pick(n, cands):
    for c in cands:
        if n % c == 0:
            return c
    raise ValueError(f"no tile for {n} in {cands}")


def _mod_row(blk, nb, n_batch):
    return jnp.where(blk % nb == 0, n_batch, blk // nb)


def _split3(x):
    hi = x.astype(BF16)
    r1 = x - hi.astype(F32)
    mid = r1.astype(BF16)
    lo = (r1 - mid.astype(F32)).astype(BF16)
    return hi, mid, lo


def _dot(a, b):
    return jnp.dot(a, b, preferred_element_type=F32)


def _dot_nt(a, b):
    return lax.dot_general(a, b, (((1,), (1,)), ((), ())), preferred_element_type=F32)


def _dot_tn(a, b):
    return lax.dot_general(a, b, (((0,), (0,)), ((), ())), preferred_element_type=F32)


def _head_sum(x, e):
    hi = x.astype(BF16)
    lo = (x - hi.astype(F32)).astype(BF16)
    w = e.shape[0]
    return jnp.concatenate(
        [_dot(hi[:, s:s + w], e) + _dot(lo[:, s:s + w], e) for s in range(0, x.shape[1], w)], axis=1)


def _dot_e3(e, x):
    hi, mid, lo = _split3(x)
    return _dot(e, hi) + _dot(e, mid) + _dot(e, lo)


def _dot_hl(x, w):
    xh = x.astype(BF16)
    xl = (x - xh.astype(F32)).astype(BF16)
    wh = w.astype(BF16)
    wl = (w - wh.astype(F32)).astype(BF16)
    return _dot(xh, wh) + _dot(xh, wl) + _dot(xl, wh)


def _mod_kernel(c_ref, w_ref, b_ref, o_ref):
    c = c_ref[...]
    s = (c * jax.nn.sigmoid(c)).astype(BF16)
    o_ref[0] = _dot(s, w_ref[0].astype(BF16)) + b_ref[0]


def _mod_tables(c_all, w_ada, b_ada):
    nl, d, n6 = w_ada.shape
    mr = c_all.shape[0]
    tn = _pick(n6, (1024, 512, 256, 128))
    return pl.pallas_call(
        _mod_kernel,
        grid=(nl, n6 // tn),
        in_specs=[
            pl.BlockSpec((mr, d), lambda l, j: (0, 0)),
            pl.BlockSpec((1, d, tn), lambda l, j: (l, 0, j)),
            pl.BlockSpec((1, 1, tn), lambda l, j: (l, 0, j)),
        ],
        out_specs=pl.BlockSpec((1, mr, tn), lambda l, j: (l, 0, j)),
        out_shape=jax.ShapeDtypeStruct((nl, mr, n6), F32),
        compiler_params=_cparams(("parallel", "parallel")),
        name="adaln_table",
    )(c_all, w_ada, b_ada.reshape(nl, 1, n6))


def _norm_kernel(x_ref, g_ref, shift_ref, scale_ref, o_ref, *, nb, n_batch):
    row = _mod_row(pl.program_id(0), nb, n_batch)
    x = x_ref[...]
    y = x * lax.rsqrt(jnp.mean(x * x, axis=-1, keepdims=True) + 1e-6) * g_ref[...]
    y = y * (1.0 + scale_ref[pl.ds(row, 1), :]) + shift_ref[pl.ds(row, 1), :]
    o_ref[...] = y.astype(o_ref.dtype)


def _norm_mod(x, g, mod, shift_idx, scale_idx, nb, n_batch):
    n, d = x.shape
    mr = mod.shape[0]
    return pl.pallas_call(
        functools.partial(_norm_kernel, nb=nb, n_batch=n_batch),
        grid=(n // RB,),
        in_specs=[
            pl.BlockSpec((RB, d), lambda i: (i, 0)),
            pl.BlockSpec((1, d), lambda i: (0, 0)),
            pl.BlockSpec((mr, d), lambda i: (0, shift_idx)),
            pl.BlockSpec((mr, d), lambda i: (0, scale_idx)),
        ],
        out_specs=pl.BlockSpec((RB, d), lambda i: (i, 0)),
        out_shape=jax.ShapeDtypeStruct((n, d), BF16),
        compiler_params=_cparams(("parallel",)),
        name="norm_mod",
    )(x, g.reshape(1, d), mod, mod)


def _final_norm_kernel(x_ref, g_ref, o_ref):
    x = x_ref[...]
    o_ref[...] = x * lax.rsqrt(jnp.mean(x * x, axis=-1, keepdims=True) + 1e-6) * g_ref[...]


def _final_norm(x, g, nb, n_batch):
    n, d = x.shape
    nlb = nb - 1
    return pl.pallas_call(
        _final_norm_kernel,
        grid=(n_batch, nlb),
        in_specs=[
            pl.BlockSpec((RB, d), lambda b, i: (b * nb + i + 1, 0)),
            pl.BlockSpec((1, d), lambda b, i: (0, 0)),
        ],
        out_specs=pl.BlockSpec((RB, d), lambda b, i: (b * nlb + i, 0)),
        out_shape=jax.ShapeDtypeStruct((n_batch * nlb * RB, d), F32),
        compiler_params=_cparams(("parallel", "parallel")),
        name="final_norm",
    )(x, g.reshape(1, d))


def _mm_plain_kernel(a_ref, w_ref, o_ref):
    o_ref[...] = _dot(a_ref[...], w_ref[...]).astype(o_ref.dtype)


def _mm_tiles(n, k, m):
    tn = _pick(m, (1024, 512, 384, 256, 128)) if k <= 2048 else _pick(m, (512, 256, 128))
    tm = _pick(n, (1024, 512, 256))
    return tm, tn


def _mm(a, w, out_dtype):
    n, k = a.shape
    m = w.shape[1]
    tm, tn = _mm_tiles(n, k, m)
    return pl.pallas_call(
        _mm_plain_kernel,
        grid=(n // tm, m // tn),
        in_specs=[
            pl.BlockSpec((tm, k), lambda i, j: (i, 0)),
            pl.BlockSpec((k, tn), lambda i, j: (0, j)),
        ],
        out_specs=pl.BlockSpec((tm, tn), lambda i, j: (i, j)),
        out_shape=jax.ShapeDtypeStruct((n, m), out_dtype),
        compiler_params=_cparams(("parallel", "parallel")),
        name="mm_plain",
    )(a, w)


def _mm_rope_kernel(a_ref, w_ref, cos_ref, sin_ref, o_ref, *, scale):
    acc = _dot(a_ref[...], w_ref[...])
    lane = lax.broadcasted_iota(jnp.int32, (1, LANES), 1)
    first_half = (lane % (C_HEAD // 2)) < (C_HEAD // 4)
    cos = cos_ref[...]
    sin = sin_ref[...]
    for s in range(acc.shape[1] // LANES):
        u = acc[:, s * LANES:(s + 1) * LANES]
        partner = jnp.where(first_half,
                            pltpu.roll(u, LANES - C_HEAD // 4, 1),
                            pltpu.roll(u, C_HEAD // 4, 1))
        o_ref[:, s * LANES:(s + 1) * LANES] = ((u * cos + partner * sin) * scale).astype(o_ref.dtype)


def _mm_rope(a, w, cos_t, sin_t, scale, out_dtype):
    n, k = a.shape
    m = w.shape[1]
    tm, tn = _mm_tiles(n, k, m)
    return pl.pallas_call(
        functools.partial(_mm_rope_kernel, scale=scale),
        grid=(n // tm, m // tn),
        in_specs=[
            pl.BlockSpec((tm, k), lambda i, j: (i, 0)),
            pl.BlockSpec((k, tn), lambda i, j: (0, j)),
            pl.BlockSpec((tm, LANES), lambda i, j: (i, 0)),
            pl.BlockSpec((tm, LANES), lambda i, j: (i, 0)),
        ],
        out_specs=pl.BlockSpec((tm, tn), lambda i, j: (i, j)),
        out_shape=jax.ShapeDtypeStruct((n, m), out_dtype),
        compiler_params=_cparams(("parallel", "parallel")),
        name="mm_rope",
    )(a, w, cos_t, sin_t)


def _mm_swiglu_kernel(a_ref, wg_ref, wu_ref, o_ref):
    a = a_ref[...]
    g = _dot(a, wg_ref[...])
    u = _dot(a, wu_ref[...])
    o_ref[...] = (g * jax.nn.sigmoid(g) * u).astype(o_ref.dtype)


def _mm_swiglu(a, w, out_dtype):
    n, k = a.shape
    f = w.shape[1] // 2
    tm, tn = _mm_tiles(n, k, f)
    nj = f // tn
    return pl.pallas_call(
        _mm_swiglu_kernel,
        grid=(n // tm, nj),
        in_specs=[
            pl.BlockSpec((tm, k), lambda i, j: (i, 0)),
            pl.BlockSpec((k, tn), lambda i, j: (0, j)),
            pl.BlockSpec((k, tn), lambda i, j: (0, j + nj)),
        ],
        out_specs=pl.BlockSpec((tm, tn), lambda i, j: (i, j)),
        out_shape=jax.ShapeDtypeStruct((n, f), out_dtype),
        compiler_params=_cparams(("parallel", "parallel")),
        name="mm_swiglu",
    )(a, w, w)


def _mm_res_kernel(a_ref, w_ref, res_ref, gate_ref, o_ref, *, nb, n_batch):
    acc = _dot(a_ref[...], w_ref[...])
    tm = acc.shape[0]
    for s in range(tm // RB):
        row = _mod_row(pl.program_id(0) * (tm // RB) + s, nb, n_batch)
        gate = gate_ref[pl.ds(row, 1), :]
        rows = slice(s * RB, (s + 1) * RB)
        o_ref[rows, :] = res_ref[rows, :] + gate * acc[rows, :]


def _mm_res(a, w, res, mod, gate_idx, nb, n_batch):
    n, k = a.shape
    m = w.shape[1]
    mr = mod.shape[0]
    tm, tn = _mm_tiles(n, k, m)
    goff = gate_idx * (m // tn)
    return pl.pallas_call(
        functools.partial(_mm_res_kernel, nb=nb, n_batch=n_batch),
        grid=(n // tm, m // tn),
        in_specs=[
            pl.BlockSpec((tm, k), lambda i, j: (i, 0)),
            pl.BlockSpec((k, tn), lambda i, j: (0, j)),
            pl.BlockSpec((tm, tn), lambda i, j: (i, j)),
            pl.BlockSpec((mr, tn), lambda i, j: (0, goff + j)),
        ],
        out_specs=pl.BlockSpec((tm, tn), lambda i, j: (i, j)),
        out_shape=jax.ShapeDtypeStruct((n, m), F32),
        compiler_params=_cparams(("parallel", "parallel")),
        name="mm_residual",
    )(a, w, res, mod)


def _mm_res_norm_kernel(a_ref, w_ref, res_ref, gate_ref, g_ref, shift_ref, scale_ref, x_ref, h_ref, *, nb, n_batch):
    acc = _dot(a_ref[...], w_ref[...])
    tm = acc.shape[0]
    for s in range(tm // RB):
        row = _mod_row(pl.program_id(0) * (tm // RB) + s, nb, n_batch)
        rows = slice(s * RB, (s + 1) * RB)
        x = res_ref[rows, :] + gate_ref[pl.ds(row, 1), :] * acc[rows, :]
        x_ref[rows, :] = x
        y = x * lax.rsqrt(jnp.mean(x * x, axis=-1, keepdims=True) + 1e-6) * g_ref[...]
        h_ref[rows, :] = (y * (1.0 + scale_ref[pl.ds(row, 1), :]) + shift_ref[pl.ds(row, 1), :]).astype(h_ref.dtype)


def _mm_res_norm(a, w, res, mod, gate_idx, g_next, mod_next, shift_idx, scale_idx, nb, n_batch):
    n, k = a.shape
    d = w.shape[1]
    mr = mod.shape[0]
    tm = _pick(n, (512, 256)) if k <= 2048 else RB
    row = lambda i: (i, 0)
    return pl.pallas_call(
        functools.partial(_mm_res_norm_kernel, nb=nb, n_batch=n_batch),
        grid=(n // tm,),
        in_specs=[
            pl.BlockSpec((tm, k), row),
            pl.BlockSpec((k, d), lambda i: (0, 0), pipeline_mode=pl.Buffered(1)),
            pl.BlockSpec((tm, d), row),
            pl.BlockSpec((mr, d), lambda i: (0, gate_idx)),
            pl.BlockSpec((1, d), lambda i: (0, 0)),
            pl.BlockSpec((mr, d), lambda i: (0, shift_idx)),
            pl.BlockSpec((mr, d), lambda i: (0, scale_idx)),
        ],
        out_specs=[pl.BlockSpec((tm, d), row), pl.BlockSpec((tm, d), row)],
        out_shape=[jax.ShapeDtypeStruct((n, d), F32), jax.ShapeDtypeStruct((n, d), BF16)],
        compiler_params=_cparams(("parallel",)),
        name="mm_residual_norm",
    )(a, w, res, mod, g_next.reshape(1, d), mod_next, mod_next)


def _merge_kernel(ya_ref, yb_ref, yc_ref, w_ref, za_ref, zb_ref, zc_ref, o_ref):
    acc = jax.nn.sigmoid(za_ref[...]) * _dot(ya_ref[...], w_ref[0])
    acc += jax.nn.sigmoid(zb_ref[...]) * _dot(yb_ref[...], w_ref[1])
    acc += jax.nn.sigmoid(zc_ref[...]) * _dot(yc_ref[...], w_ref[2])
    o_ref[...] = acc.astype(o_ref.dtype)


def _merge(ya, yb, yc, w_branch, z_mix):
    n, kw = ya.shape
    d = w_branch.shape[2]
    tm = _pick(n, (1024, 512, 256))
    tn = _pick(d, (512, 256, 128))
    nj = d // tn
    yspec = pl.BlockSpec((tm, kw), lambda i, j: (i, 0))
    return pl.pallas_call(
        _merge_kernel,
        grid=(n // tm, nj),
        in_specs=[
            yspec, yspec, yspec,
            pl.BlockSpec((N_BRANCH, kw, tn), lambda i, j: (0, 0, j)),
            pl.BlockSpec((tm, tn), lambda i, j: (i, j)),
            pl.BlockSpec((tm, tn), lambda i, j: (i, nj + j)),
            pl.BlockSpec((tm, tn), lambda i, j: (i, 2 * nj + j)),
        ],
        out_specs=pl.BlockSpec((tm, tn), lambda i, j: (i, j)),
        out_shape=jax.ShapeDtypeStruct((n, d), BF16),
        compiler_params=_cparams(("parallel", "parallel")),
        name="merge",
    )(ya, yb, yc, w_branch, z_mix, z_mix, z_mix)


def _prep_kernel(zr_ref, zp_ref, zn_ref, zs_ref, conv_ref, w0_ref, wup_ref, a0_ref, aup_ref,
                 kk_ref, ka_ref, e_ref,
                 r_ref, v_ref, nkk_ref, kd_ref, bd_ref, lw_ref, up_ref, dn_ref, *, seq, tc):
    rt = zr_ref.shape[0]
    aw = r_ref.shape[1]
    pos = (pl.program_id(0) * rt) % seq
    prev_ok = jnp.logical_and(pos != 0, pos != tc)
    end = pos + rt
    next_ok = jnp.logical_and(end != seq, end != tc)
    u = zr_ref[...]
    up_ref[...] = pltpu.roll(u, 1, 0)
    up_ref[0:1, :] = jnp.where(prev_ok, zp_ref[HALO - 1:HALO, :], 0.0)
    dn_ref[...] = pltpu.roll(u, rt - 1, 0)
    dn_ref[rt - 1:rt, :] = jnp.where(next_ok, zn_ref[0:1, :], 0.0)
    c = up_ref[...] * conv_ref[0:1, :] + u * conv_ref[1:2, :] + dn_ref[...] * conv_ref[2:3, :]
    r = c[:, 0:aw]
    k = c[:, aw:2 * aw]
    v = c[:, 2 * aw:3 * aw]
    r_ref[...] = r
    v_ref[...] = v
    kx = k * kk_ref[...]
    kk = kx * lax.rsqrt(jnp.maximum(_head_sum(kx * kx, e_ref[...]), 1e-24))
    nkk_ref[...] = -kk
    zs = zs_ref[...]
    ka = ka_ref[...]
    for d in range(2):
        zd = zs[:, d * A_RANK:(d + 1) * A_RANK]
        w_raw = w0_ref[d:d + 1, :] + _dot_hl(jnp.tanh(zd), wup_ref[d])
        lw_ref[d] = -math.exp(-0.5) * jax.nn.sigmoid(w_raw)
        za = zs[:, 2 * A_RANK + d * A_RANK:2 * A_RANK + (d + 1) * A_RANK]
        a = jax.nn.sigmoid(a0_ref[d:d + 1, :] + _dot_hl(za, aup_ref[d]))
        kd_ref[d] = k * (1.0 + (a - 1.0) * ka)
        bd_ref[d] = kk * a


def _rwkv_prepare(z_rkv, z_small, conv_w, w0, w_up, a0, a_up, k_k, k_a, e_head, seq, tc):
    n, w3 = z_rkv.shape
    aw = w3 // 3
    rt = RB
    nh = rt // HALO
    last = n // HALO - 1
    full2 = lambda i: (0, 0)
    full3 = lambda i: (0, 0, 0)
    row = jax.ShapeDtypeStruct((n, aw), F32)
    drow = jax.ShapeDtypeStruct((2, n, aw), F32)
    return pl.pallas_call(
        functools.partial(_prep_kernel, seq=seq, tc=tc),
        grid=(n // rt,),
        in_specs=[
            pl.BlockSpec((rt, w3), lambda i: (i, 0)),
            pl.BlockSpec((HALO, w3), lambda i: (jnp.maximum(i * nh - 1, 0), 0)),
            pl.BlockSpec((HALO, w3), lambda i: (jnp.minimum((i + 1) * nh, last), 0)),
            pl.BlockSpec((rt, z_small.shape[1]), lambda i: (i, 0)),
            pl.BlockSpec(conv_w.shape, full2),
            pl.BlockSpec(w0.shape, full2),
            pl.BlockSpec(w_up.shape, full3),
            pl.BlockSpec(a0.shape, full2),
            pl.BlockSpec(a_up.shape, full3),
            pl.BlockSpec((1, aw), full2),
            pl.BlockSpec((1, aw), full2),
            pl.BlockSpec(e_head.shape, full2),
        ],
        out_specs=[
            pl.BlockSpec((rt, aw), lambda i: (i, 0)),
            pl.BlockSpec((rt, aw), lambda i: (i, 0)),
            pl.BlockSpec((rt, aw), lambda i: (i, 0)),
            pl.BlockSpec((2, rt, aw), lambda i: (0, i, 0)),
            pl.BlockSpec((2, rt, aw), lambda i: (0, i, 0)),
            pl.BlockSpec((2, rt, aw), lambda i: (0, i, 0)),
        ],
        out_shape=[row, row, row, drow, drow, drow],
        scratch_shapes=[pltpu.VMEM((rt, w3), F32), pltpu.VMEM((rt, w3), F32)],
        compiler_params=_cparams(("parallel",)),
        name="rwkv_prepare",
    )(z_rkv, z_rkv, z_rkv, z_small, conv_w, w0, w_up, a0, a_up,
      k_k.reshape(1, aw), k_a.reshape(1, aw), e_head)


def _wkv_kernel(rf_ref, vf_ref, af_ref, kf_ref, bf_ref, lwf_ref, rb_ref, vb_ref, ab_ref, kb_ref, bb_ref, lwb_ref,
                yf_ref, yb_ref, s_ref):
    c = rf_ref.shape[0]
    c2 = 2 * c
    n_pair = rf_ref.shape[1] // PAIR

    @pl.when(pl.program_id(1) == 0)
    def _():
        s_ref[...] = jnp.zeros_like(s_ref)

    ti = lax.broadcasted_iota(jnp.int32, (c, c), 0)
    si = lax.broadcasted_iota(jnp.int32, (c, c), 1)
    i2 = lax.broadcasted_iota(jnp.int32, (c2, c2), 0)
    j2 = lax.broadcasted_iota(jnp.int32, (c2, c2), 1)
    col_minus_row = j2 % c - i2 % c
    xr = i2 ^ j2
    eye = (i2 == j2).astype(F32)
    lane = lax.broadcasted_iota(jnp.int32, (1, PAIR), 1)
    m0 = (lane < A_HEAD).astype(F32)
    m1 = 1.0 - m0

    def stack(x):
        return jnp.concatenate([x * m0, x * m1], axis=0)

    zl, zr, v_s, bk_end, dec_end, strict, incl, out = [], [], [], [], [], [], [], []
    for r_ref, v_ref, a_ref, k_ref, b_ref, lw_ref, y_ref, sgn in (
            (rf_ref, vf_ref, af_ref, kf_ref, bf_ref, lwf_ref, yf_ref, 1),
            (rb_ref, vb_ref, ab_ref, kb_ref, bb_ref, lwb_ref, yb_ref, -1)):
        cum = jnp.where((si - ti) * sgn <= 0, 1.0, 0.0).astype(BF16)
        cs_all = _dot_e3(cum, lw_ref[...])
        last = c - 1 if sgn > 0 else 0
        for p in range(n_pair):
            cols = slice(p * PAIR, (p + 1) * PAIR)
            lw = lw_ref[:, cols]
            cs = cs_all[:, cols]
            cs_end = cs[last:last + 1, :]
            g_inv = jnp.exp(-cs)
            g_end = jnp.exp(cs_end - cs)
            kd = k_ref[:, cols]
            bd = b_ref[:, cols]
            a_t = stack(a_ref[:, cols] * jnp.exp(cs - lw))
            r_t = stack(r_ref[:, cols] * jnp.exp(cs))
            zl.append(jnp.concatenate([a_t, r_t], axis=0).astype(BF16))
            zr.append(jnp.concatenate([stack(bd * g_inv), stack(kd * g_inv)], axis=0).astype(BF16))
            v_s.append(stack(v_ref[:, cols]).astype(BF16))
            bk_end.append(jnp.concatenate([stack(bd * g_end), stack(kd * g_end)], axis=0).astype(BF16))
            dec_end.append(jnp.exp(cs_end))
            strict.append(col_minus_row * sgn < 0)
            incl.append(col_minus_row * sgn <= 0)
            out.append((y_ref, cols))

    ch = range(len(zl))
    aa = [_dot_nt(zl[i], zr[i]) for i in ch]
    s0 = [s_ref[i] for i in ch]
    zs = [_dot_nt(zl[i], s0[i].astype(BF16)) for i in ch]
    a_ab = [jnp.where(strict[i], aa[i][0:c2, 0:c2], 0.0) for i in ch]
    av = [_dot(jnp.concatenate([jnp.where(strict[i], aa[i][0:c2, c2:2 * c2], 0.0),
                                jnp.where(incl[i], aa[i][c2:2 * c2, c2:2 * c2], 0.0)], axis=0).astype(BF16),
               v_s[i]) for i in ch]
    a_rb = [jnp.where(incl[i], aa[i][c2:2 * c2, 0:c2], 0.0).astype(BF16) for i in ch]

    n1 = [jnp.where(xr < 8, a_ab[i], 0.0).astype(BF16) for i in ch]
    n2 = [_dot(n1[i], n1[i]).astype(BF16) for i in ch]
    t = [eye + n1[i].astype(F32) for i in ch]
    t = [t[i] + _dot(t[i].astype(BF16), n2[i]) for i in ch]
    n4 = [_dot(n2[i], n2[i]).astype(BF16) for i in ch]
    t = [t[i] + _dot(t[i].astype(BF16), n4[i]) for i in ch]
    m = 8
    while m < c:
        level = jnp.logical_and(xr >= m, xr < 2 * m)
        tb = [t[i].astype(BF16) for i in ch]
        x = [_dot(jnp.where(level, a_ab[i], 0.0).astype(BF16), tb[i]).astype(BF16) for i in ch]
        t = [t[i] + _dot(tb[i], x[i]) for i in ch]
        m *= 2

    u = [_dot(t[i].astype(BF16), (zs[i][0:c2] + av[i][0:c2]).astype(BF16)).astype(BF16) for i in ch]
    yu = [_dot(a_rb[i], u[i]) for i in ch]
    ds = [_dot_tn(jnp.concatenate([u[i], v_s[i]], axis=0), bk_end[i]) for i in ch]
    for i in ch:
        y_s = zs[i][c2:2 * c2] + av[i][c2:2 * c2] + yu[i]
        y_ref, cols = out[i]
        y_ref[:, cols] = y_s[0:c] + y_s[c:c2]
        s_ref[i] = s0[i] * dec_end[i] + ds[i]


def _wkv(r, v, nkk, kd, bd, lw, n_batch, seq, tc):
    n, aw = r.shape
    c = CHUNK
    nch = seq // c
    ncc = tc // c

    def fwd_blk(b, i):
        return b * nch + i

    def bwd_blk(b, i):
        return b * nch + jnp.where(i < ncc, ncc - 1 - i, nch - 1 - (i - ncc))

    def specs(blk, d):
        shared = pl.BlockSpec((c, aw), lambda b, i: (blk(b, i), 0))
        perdir = pl.BlockSpec((None, c, aw), lambda b, i: (d, blk(b, i), 0))
        return [shared, shared, shared, perdir, perdir, perdir], shared

    in_f, out_f = specs(fwd_blk, 0)
    in_b, out_b = specs(bwd_blk, 1)
    y = jax.ShapeDtypeStruct((n, aw), F32)
    return pl.pallas_call(
        _wkv_kernel,
        grid=(n_batch, nch),
        in_specs=in_f + in_b,
        out_specs=[out_f, out_b],
        out_shape=[y, y],
        scratch_shapes=[pltpu.VMEM((2 * (aw // PAIR), PAIR, PAIR), F32)],
        compiler_params=_cparams(("parallel", "arbitrary")),
        name="wkv_scan",
    )(r, v, nkk, kd, bd, lw, r, v, nkk, kd, bd, lw)


def _rwkv_out_kernel(yf_ref, yb_ref, r_ref, v_ref, kd_ref, zs_ref, rk_ref, lw_ref, lb_ref, gup_ref, e_ref, o_ref):
    e = e_ref[...]
    inv = 1.0 / A_HEAD
    y = yf_ref[...] + yb_ref[...]
    mu = _head_sum(y, e) * inv
    yc = y - mu
    var = _head_sum(yc * yc, e) * inv
    yh = yc * lax.rsqrt(var + LNX_EPS) * lw_ref[...] + lb_ref[...]
    bonus = _head_sum(r_ref[...] * (kd_ref[0] + kd_ref[1]) * rk_ref[...], e) * v_ref[...]
    zg = zs_ref[...][:, 2 * 2 * A_RANK:2 * 2 * A_RANK + A_GATE_RANK]
    g = _dot_hl(jax.nn.sigmoid(zg), gup_ref[...])
    o_ref[...] = ((yh + bonus) * g).astype(o_ref.dtype)


def _rwkv_output(yf, yb, r, v, kd, z_small, r_k, lnx_w, lnx_b, gate_up, e_head):
    n, aw = r.shape
    rt = RB
    rowspec = pl.BlockSpec((rt, aw), lambda i: (i, 0))
    dspec = pl.BlockSpec((2, rt, aw), lambda i: (0, i, 0))
    vec = pl.BlockSpec((1, aw), lambda i: (0, 0))
    return pl.pallas_call(
        _rwkv_out_kernel,
        grid=(n // rt,),
        in_specs=[
            rowspec, rowspec, rowspec, rowspec, dspec,
            pl.BlockSpec((rt, z_small.shape[1]), lambda i: (i, 0)),
            vec, vec, vec,
            pl.BlockSpec(gate_up.shape, lambda i: (0, 0)),
            pl.BlockSpec(e_head.shape, lambda i: (0, 0)),
        ],
        out_specs=rowspec,
        out_shape=jax.ShapeDtypeStruct((n, aw), BF16),
        compiler_params=_cparams(("parallel",)),
        name="rwkv_output",
    )(yf, yb, r, v, kd, z_small, r_k.reshape(1, aw), lnx_w.reshape(1, aw), lnx_b.reshape(1, aw),
      gate_up, e_head)


def _pool_kernel(u_ref, w_ref, ps_ref, o_ref, pad_ref, *, tc):
    seq = u_ref.shape[0]
    t_lat = seq - tc
    half_max = max(B_WINDOWS) // 2
    lat0 = half_max + tc + 2 * half_max
    gw = u_ref.shape[1]
    zeros = jnp.zeros((2 * half_max, gw), F32)
    pad_ref[0:half_max, :] = zeros[0:half_max]
    pad_ref[half_max:half_max + tc, :] = u_ref[0:tc, :]
    pad_ref[half_max + tc:lat0, :] = zeros
    pad_ref[lat0:lat0 + t_lat, :] = u_ref[tc:seq, :]
    pad_ref[lat0 + t_lat:lat0 + t_lat + half_max, :] = zeros[0:half_max]

    def segment(start, n, win):
        half = win // 2
        acc = pad_ref[start - half:start - half + n, :]
        for j in range(1 - half, half):
            acc = acc + pad_ref[start + j:start + j + n, :]
        t = lax.broadcasted_iota(jnp.int32, (n, gw), 0)
        cnt = jnp.minimum(t + half, n) - jnp.maximum(t - half, 0)
        return acc / cnt.astype(F32) - pad_ref[start:start + n, :]

    for gi, win in enumerate(B_WINDOWS):
        @pl.when(pl.program_id(1) == gi)
        def _(win=win):
            w = w_ref[0]
            ps = ps_ref[...]
            o_ref[0:tc, :] = (_dot(segment(half_max, tc, win).astype(BF16), w) * ps).astype(o_ref.dtype)
            o_ref[tc:seq, :] = (_dot(segment(lat0, t_lat, win).astype(BF16), w) * ps).astype(o_ref.dtype)


def _pool(z_pool, pool_w, pool_scale, n_batch, seq, tc):
    n, bw = z_pool.shape
    ng = len(B_WINDOWS)
    gw = bw // ng
    pad_rows = seq + 4 * (max(B_WINDOWS) // 2)
    return pl.pallas_call(
        functools.partial(_pool_kernel, tc=tc),
        grid=(n_batch, ng),
        in_specs=[
            pl.BlockSpec((seq, gw), lambda b, g: (b, g)),
            pl.BlockSpec((1, gw, gw), lambda b, g: (g, 0, 0)),
            pl.BlockSpec((1, gw), lambda b, g: (0, g)),
        ],
        out_specs=pl.BlockSpec((seq, gw), lambda b, g: (b, g)),
        out_shape=jax.ShapeDtypeStruct((n, bw), BF16),
        scratch_shapes=[pltpu.VMEM((pad_rows, gw), F32)],
        compiler_params=_cparams(("parallel", "parallel")),
        name="pool_mixer",
    )(z_pool, pool_w, pool_scale.reshape(1, bw))


def _attn_kernel(q_ref, k_ref, v_ref, lam_ref, g_ref, o_ref, *, tc, lam_init):
    lq = lam_ref[...]
    lam = (jnp.exp(jnp.sum(lq[0:1] * lq[1:2], axis=-1, keepdims=True))
           - jnp.exp(jnp.sum(lq[2:3] * lq[3:4], axis=-1, keepdims=True)) + lam_init)
    lane = lax.broadcasted_iota(jnp.int32, (1, C_VHEAD), 1)
    g = g_ref[...] * (1.0 - lam_init)

    ones = jnp.ones((KEY_TILE, C_VHEAD), BF16)

    def attend(nk):
        for hh in range(q_ref.shape[1] // C_VHEAD):
            hl = slice(hh * C_VHEAD, (hh + 1) * C_VHEAD)
            q = q_ref[:, hl]
            zero = jnp.zeros_like(q)
            rq = q.shape[0]
            qq = jnp.concatenate([jnp.where(lane < C_HEAD, q, zero), jnp.where(lane < C_HEAD, zero, q)], axis=0)
            m = acc = None
            for t in range(nk // KEY_TILE):
                rows = slice(t * KEY_TILE, (t + 1) * KEY_TILE)
                s = _dot_nt(qq, k_ref[rows, hl])
                mt = jnp.max(s, axis=-1, keepdims=True)
                m_new = mt if m is None else jnp.maximum(m, mt)
                e = jnp.exp(s - m_new).astype(BF16)
                pv = _dot(e, jnp.concatenate([v_ref[rows, hl], ones], axis=1))
                acc = pv if m is None else jnp.exp(m - m_new) * acc + pv
                m = m_new
            p = acc[:, 0:C_VHEAD] / acc[:, C_VHEAD:2 * C_VHEAD]
            o = p[0:rq] - lam * p[rq:2 * rq]
            o = o * lax.rsqrt(jnp.mean(o * o, axis=-1, keepdims=True) + 1e-5) * g
            o_ref[:, hl] = o.astype(o_ref.dtype)

    @pl.when(pl.program_id(2) == 0)
    def _():
        attend(tc)

    @pl.when(pl.program_id(2) != 0)
    def _():
        attend(k_ref.shape[0])


def _attention(q, k, v, lam_qk, subln_g, lam_init, n_batch, seq, tc):
    n, w = q.shape
    gw = _pick(w, (ATTN_HEADS_PER_STEP * C_VHEAD, C_VHEAD))
    nb = seq // RB
    kv = pl.BlockSpec((seq, gw), lambda b, h, i: (b, h))
    qo = pl.BlockSpec((RB, gw), lambda b, h, i: (b * nb + i, h))
    return pl.pallas_call(
        functools.partial(_attn_kernel, tc=tc, lam_init=lam_init),
        grid=(n_batch, w // gw, nb),
        in_specs=[
            qo, kv, kv,
            pl.BlockSpec(lam_qk.shape, lambda b, h, i: (0, 0)),
            pl.BlockSpec((1, C_VHEAD), lambda b, h, i: (0, 0)),
        ],
        out_specs=qo,
        out_shape=jax.ShapeDtypeStruct((n, w), BF16),
        compiler_params=_cparams(("parallel", "parallel", "parallel")),
        name="diff_attention",
    )(q, k, v, lam_qk, subln_g.reshape(1, C_VHEAD))


def _rope_tables(n_batch, t_lat, tc):
    n_freq = C_HEAD // 4
    inv = ROPE_BASE ** (-jnp.arange(n_freq, dtype=F32) / n_freq)
    rows = t_lat // GRID_W
    t_row = jnp.repeat(jnp.arange(rows, dtype=F32), GRID_W)
    t_col = jnp.tile(jnp.arange(GRID_W, dtype=F32), rows)
    ang_r = t_row[:, None] * inv
    ang_c = t_col[:, None] * inv
    cos = jnp.concatenate([jnp.cos(ang_r)] * 2 + [jnp.cos(ang_c)] * 2, axis=1)
    sin = jnp.concatenate([-jnp.sin(ang_r), jnp.sin(ang_r), -jnp.sin(ang_c), jnp.sin(ang_c)], axis=1)
    cos = jnp.concatenate([jnp.ones((tc, C_HEAD), F32), cos], axis=0)
    sin = jnp.concatenate([jnp.zeros((tc, C_HEAD), F32), sin], axis=0)
    cos = jnp.tile(jnp.tile(cos, (1, LANES // C_HEAD)), (n_batch, 1))
    sin = jnp.tile(jnp.tile(sin, (1, LANES // C_HEAD)), (n_batch, 1))
    return cos, sin


def kernel(x, c, ctx, c_ctx, w_ada, b_ada, norm_g, w_in, rkv_conv, decay_w0, decay_up, iclr_a0, iclr_up, gate_up, k_k, k_a, r_k, lnx_w, lnx_b, pool_w, pool_scale, lam_qk, subln_g, w_branch, w_out, w_ffn_in, w_ffn_out, final_g):
    n_batch, t_lat, d = x.shape
    tc = ctx.shape[1]
    depth = w_ada.shape[0]
    aw = k_k.shape[1]
    assert tc == RB and t_lat % RB == 0 and t_lat % GRID_W == 0 and aw % PAIR == 0
    seq = tc + t_lat
    nb = seq // RB
    n = n_batch * seq

    xa = jnp.concatenate([ctx, x], axis=1).reshape(n, d)
    mr = -(-(n_batch + 1) // SUBLANES) * SUBLANES
    c_all = jnp.zeros((mr, d), F32).at[:n_batch].set(c).at[n_batch].set(c_ctx)
    mods = _mod_tables(c_all, w_ada, b_ada)
    cos_t, sin_t = _rope_tables(n_batch, t_lat, tc)
    hid = lax.broadcasted_iota(jnp.int32, (MXU_COLS, MXU_COLS), 0) // A_HEAD
    e_head = (hid == hid.T).astype(BF16)

    o_small = 3 * aw
    o_pool = o_small + 2 * 2 * A_RANK + A_GATE_RANK
    o_q = o_pool + aw
    o_k = o_q + aw
    o_v = o_k + aw
    o_mix = o_v + aw

    h = _norm_mod(xa, norm_g[0, 0], mods[0], 0, 1, nb, n_batch)
    for l in range(depth):
        lam_init = 0.8 - 0.6 * math.exp(-0.3 * l)
        mod = mods[l]

        def wi(lo, hi):
            return w_in[l, :, lo:hi].astype(BF16)

        z_rkv = _mm(h, wi(0, o_small), F32)
        z_small = _mm(h, wi(o_small, o_pool), F32)
        z_pool = _mm(h, wi(o_pool, o_q), F32)
        q = _mm_rope(h, wi(o_q, o_k), cos_t, sin_t, C_HEAD ** -0.5, BF16)
        k = _mm_rope(h, wi(o_k, o_v), cos_t, sin_t, 1.0, BF16)
        v = _mm(h, wi(o_v, o_mix), BF16)
        z_mix = _mm(h, wi(o_mix, w_in.shape[2]), F32)

        r, va, nkk, kd, bd, lw = _rwkv_prepare(
            z_rkv, z_small, rkv_conv[l], decay_w0[l], decay_up[l], iclr_a0[l], iclr_up[l],
            k_k[l], k_a[l], e_head, seq, tc)
        yf, yb_dir = _wkv(r, va, nkk, kd, bd, lw, n_batch, seq, tc)
        ya = _rwkv_output(yf, yb_dir, r, va, kd, z_small, r_k[l], lnx_w[l], lnx_b[l], gate_up[l], e_head)
        yb = _pool(z_pool, pool_w[l].astype(BF16), pool_scale[l], n_batch, seq, tc)
        yc = _attention(q, k, v, lam_qk[l], subln_g[l], lam_init, n_batch, seq, tc)

        acc = _merge(ya, yb, yc, w_branch[l].astype(BF16), z_mix)
        xa, h2 = _mm_res_norm(acc, w_out[l].astype(BF16), xa, mod, 2, norm_g[l, 1], mod, 3, 4, nb, n_batch)
        ff = _mm_swiglu(h2, w_ffn_in[l].astype(BF16), BF16)
        if l + 1 < depth:
            xa, h = _mm_res_norm(ff, w_ffn_out[l].astype(BF16), xa, mod, 5,
                                 norm_g[l + 1, 0], mods[l + 1], 0, 1, nb, n_batch)
        else:
            xa = _mm_res(ff, w_ffn_out[l].astype(BF16), xa, mod, 5, nb, n_batch)

    out = _final_norm(xa, final_g, nb, n_batch)
    return out.reshape(n_batch, t_lat, d)
```
